```python
import math
import jax, jax.numpy as jnp
from jax import lax
import numpy as np

D_MODEL = 2048
BATCH = 4
SEQ = 8192
DEPTH = 1
DEC_BATCH = 8
DEC_SEQ = 4096
PAST_LEN = 128

D_POOL = D_MODEL // 2
POOL_WINDOWS = (2, 4, 8, 16)
N_POOL_GROUPS = len(POOL_WINDOWS)
POOL_GROUP = D_POOL // N_POOL_GROUPS
D_HYENA = D_MODEL // 2
HYENA_ORDER = 2
SHORT_CONV = 3
FILTER_EMB = 33
FILTER_BANDS = (FILTER_EMB - 1) // 2
FILTER_HIDDEN = 64
DECAY_TARGET = 1e-2
FAST_DECAY_PCT = 0.3
SLOW_DECAY_PCT = 1.5
D_FF = int(math.ceil(8 * D_MODEL / 3 / 256)) * 256
D_IN = D_POOL + (HYENA_ORDER + 1) * D_HYENA + 2 * D_MODEL
EPS = 1e-6

kernel_name = "gated_pool_hyena_encoder"


def _rmsnorm(x, g):
    xf = x.astype(jnp.float32)
    y = xf * lax.rsqrt(jnp.mean(xf * xf, axis=-1, keepdims=True) + EPS)
    return (y * g.astype(jnp.float32)).astype(x.dtype)


def _centred_mean_minus_self(u, w):
    L = u.shape[1]
    cs = jnp.concatenate([jnp.zeros_like(u[:, :1]), jnp.cumsum(u, axis=1)], axis=1)
    t = jnp.arange(L)
    lo = jnp.clip(t - w // 2, 0, L)
    hi = jnp.clip(t + (w - w // 2), 0, L)
    cnt = (hi - lo).astype(jnp.float32)
    return (cs[:, hi] - cs[:, lo]) / cnt[None, :, None] - u


def _pool_mixer(u, pool_w, pool_scale):
    B, L, _ = u.shape
    ug = u.astype(jnp.float32).reshape(B, L, N_POOL_GROUPS, POOL_GROUP)
    pooled = jnp.stack(
        [_centred_mean_minus_self(ug[:, :, g], w) for g, w in enumerate(POOL_WINDOWS)], axis=2
    )
    mixed = jnp.einsum("blgc,gcd->blgd", pooled.astype(u.dtype), pool_w)
    return mixed.reshape(B, L, D_POOL) * pool_scale


def _short_conv3(u, w, b):
    up = jnp.pad(u, ((0, 0), (1, 1), (0, 0)))
    return up[:, :-2] * w[0] + up[:, 1:-1] * w[1] + up[:, 2:] * w[2] + b


def _hyena_filters(L, w1, b1, f1, w2, b2, f2, w3):
    f32 = jnp.float32
    t = jnp.linspace(0.0, 1.0, L, dtype=f32)[:, None]
    wt = 2.0 * math.pi * jnp.arange(L, dtype=f32)[:, None] / L
    bands = jnp.linspace(1e-4, FILTER_BANDS - 1, FILTER_BANDS, dtype=f32)[None, :]
    z = jnp.concatenate([t, jnp.cos(bands * wt), -jnp.sin(bands * wt)], axis=-1)
    h = jnp.sin(f1.astype(f32) * (z @ w1.astype(f32) + b1.astype(f32)))
    h = jnp.sin(f2.astype(f32) * (h @ w2.astype(f32) + b2.astype(f32)))
    h = (h @ w3.astype(f32)).reshape(L, HYENA_ORDER, 2, D_HYENA)
    max_decay = math.log(DECAY_TARGET) / FAST_DECAY_PCT
    min_decay = math.log(DECAY_TARGET) / SLOW_DECAY_PCT
    deltas = jnp.abs(jnp.linspace(min_decay, max_decay, D_HYENA, dtype=f32))
    h = h * jnp.exp(-t * deltas)[:, None, None, :]
    fwd, bwd = h[:, :, 0], h[:, :, 1]
    two = jnp.concatenate(
        [fwd[:1] + bwd[:1], fwd[1:], jnp.zeros_like(fwd[:1]), bwd[:0:-1]], axis=0
    )
    two = two * lax.rsqrt(jnp.sum(two * two, axis=0, keepdims=True) + EPS)
    return jnp.moveaxis(two, 1, 0)


def _long_conv(z, filt_f):
    L = z.shape[1]
    zf = jnp.fft.rfft(z, n=2 * L, axis=1)
    return jnp.fft.irfft(zf * filt_f[None], n=2 * L, axis=1)[:, :L]


def _hyena_mixer(u, conv_w, conv_b, w1, b1, f1, w2, b2, f2, w3, hyena_bias):
    dt = u.dtype
    L = u.shape[1]
    uc = _short_conv3(u, conv_w, conv_b).astype(jnp.float32)
    v, x1, x2 = jnp.split(uc, HYENA_ORDER + 1, axis=-1)
    filt_f = jnp.fft.rfft(_hyena_filters(L, w1, b1, f1, w2, b2, f2, w3), axis=1)
    bias = hyena_bias.astype(jnp.float32)
    z = v
    for o, gate in enumerate((x1, x2)):
        z = gate * (_long_conv(z, filt_f[o]) + bias[o] * z)
    return z.astype(dt)


def _trunk(x, g_mix, w_in, pool_w, pool_scale, conv_w, conv_b, filt_w1, filt_b1, filt_freq1,
           filt_w2, filt_b2, filt_freq2, filt_w3, hyena_bias, w_branch_a, w_branch_b, w_out,
           g_ffn, w_gate, w_up, w_down, g_final):
    s1 = D_POOL
    s2 = s1 + (HYENA_ORDER + 1) * D_HYENA
    s3 = s2 + D_MODEL
    for i in range(DEPTH):
        h = _rmsnorm(x, g_mix[i])
        p = h @ w_in[i]
        u_pool, u_hy, gate_a, gate_b = p[..., :s1], p[..., s1:s2], p[..., s2:s3], p[..., s3:]
        a = _pool_mixer(u_pool, pool_w[i], pool_scale[i])
        b = _hyena_mixer(u_hy, conv_w[i], conv_b[i], filt_w1[i], filt_b1[i], filt_freq1[i],
                         filt_w2[i], filt_b2[i], filt_freq2[i], filt_w3[i], hyena_bias[i])
        merged = (jax.nn.sigmoid(gate_a) * (a @ w_branch_a[i])
                  + jax.nn.sigmoid(gate_b) * (b @ w_branch_b[i]))
        x = x + merged @ w_out[i]
        h = _rmsnorm(x, g_ffn[i])
        x = x + (jax.nn.silu(h @ w_gate[i]) * (h @ w_up[i])) @ w_down[i]
    return _rmsnorm(x, g_final)


def setup_inputs(seed: int = 0) -> dict:
    key = jax.random.key(seed)
    ks = jax.random.split(key, 24)
    f32 = jnp.float32

    def nrm(k, shape, scale):
        return jax.random.normal(k, shape, f32) * scale

    return {
        "x_prompt": nrm(ks[0], (BATCH, SEQ, D_MODEL), 1.0),
        "x_sample": nrm(ks[1], (DEC_BATCH, DEC_SEQ, D_MODEL), 1.0),
        "g_mix": 1.0 + nrm(ks[2], (DEPTH, D_MODEL), 0.02),
        "w_in": nrm(ks[3], (DEPTH, D_MODEL, D_IN), D_MODEL ** -0.5),
        "pool_w": nrm(ks[4], (DEPTH, N_POOL_GROUPS, POOL_GROUP, POOL_GROUP), POOL_GROUP ** -0.5),
        "pool_scale": 1.0 + nrm(ks[5], (DEPTH, D_POOL), 0.02),
        "conv_w": nrm(ks[6], (DEPTH, SHORT_CONV, (HYENA_ORDER + 1) * D_HYENA), SHORT_CONV ** -0.5),
        "conv_b": nrm(ks[7], (DEPTH, (HYENA_ORDER + 1) * D_HYENA), 0.02),
        "filt_w1": nrm(ks[8], (DEPTH, FILTER_EMB, FILTER_HIDDEN), FILTER_EMB ** -0.5),
        "filt_b1": nrm(ks[9], (DEPTH, FILTER_HIDDEN), 0.02),
        "filt_freq1": 1.0 + nrm(ks[10], (DEPTH, FILTER_HIDDEN), 0.02),
        "filt_w2": nrm(ks[11], (DEPTH, FILTER_HIDDEN, FILTER_HIDDEN), FILTER_HIDDEN ** -0.5),
        "filt_b2": nrm(ks[12], (DEPTH, FILTER_HIDDEN), 0.02),
        "filt_freq2": 1.0 + nrm(ks[13], (DEPTH, FILTER_HIDDEN), 0.02),
        "filt_w3": nrm(ks[14], (DEPTH, FILTER_HIDDEN, HYENA_ORDER * 2 * D_HYENA), FILTER_HIDDEN ** -0.5),
        "hyena_bias": nrm(ks[15], (DEPTH, HYENA_ORDER, D_HYENA), 1.0),
        "w_branch_a": nrm(ks[16], (DEPTH, D_POOL, D_MODEL), D_POOL ** -0.5),
        "w_branch_b": nrm(ks[17], (DEPTH, D_HYENA, D_MODEL), D_HYENA ** -0.5),
        "w_out": nrm(ks[18], (DEPTH, D_MODEL, D_MODEL), D_MODEL ** -0.5),
        "g_ffn": 1.0 + nrm(ks[19], (DEPTH, D_MODEL), 0.02),
        "w_gate": nrm(ks[20], (DEPTH, D_MODEL, D_FF), D_MODEL ** -0.5),
        "w_up": nrm(ks[21], (DEPTH, D_MODEL, D_FF), D_MODEL ** -0.5),
        "w_down": nrm(ks[22], (DEPTH, D_FF, D_MODEL), D_FF ** -0.5),
        "g_final": 1.0 + nrm(ks[23], (D_MODEL,), 0.02),
    }


def reference(x_prompt, x_sample, g_mix, w_in, pool_w, pool_scale, conv_w, conv_b, filt_w1, filt_b1,
              filt_freq1, filt_w2, filt_b2, filt_freq2, filt_w3, hyena_bias, w_branch_a, w_branch_b,
              w_out, g_ffn, w_gate, w_up, w_down, g_final):
    y_prompt = _trunk(x_prompt, g_mix, w_in, pool_w, pool_scale, conv_w, conv_b, filt_w1, filt_b1,
                      filt_freq1, filt_w2, filt_b2, filt_freq2, filt_w3, hyena_bias, w_branch_a,
                      w_branch_b, w_out, g_ffn, w_gate, w_up, w_down, g_final)
    y_sample = _trunk(x_sample, g_mix, w_in, pool_w, pool_scale, conv_w, conv_b, filt_w1, filt_b1,
                      filt_freq1, filt_w2, filt_b2, filt_freq2, filt_w3, hyena_bias, w_branch_a,
                      w_branch_b, w_out, g_ffn, w_gate, w_up, w_down, g_final)
    return (y_prompt, y_sample)
```

```python
import functools
import math

import jax
import jax.numpy as jnp
from jax import lax
from jax.experimental import pallas as pl
from jax.experimental.pallas import tpu as pltpu

F32 = jnp.float32
MXU_DTYPE = jnp.bfloat16

EPS = 1e-6
POOL_WINDOWS = (2, 4, 8, 16)
DECAY_TARGET = 1e-2
FAST_DECAY_PCT = 0.3
SLOW_DECAY_PCT = 1.5

LANES = 128
NF = 128
HALO = 16
VMEM_LIMIT = 56 * 1024 * 1024


def _pick(n, pref, mult=8):
    best = None
    for d in range(mult, min(n, pref) + 1, mult):
        if n % d == 0:
            best = d
    return best if best is not None else n


def _params(*sem):
    return pltpu.CompilerParams(dimension_semantics=sem, vmem_limit_bytes=VMEM_LIMIT)


def _dot(a, b, **kw):
    return jnp.dot(a, b, preferred_element_type=F32, **kw)


def _rms(x):
    return x * lax.rsqrt(jnp.mean(x * x, axis=-1, keepdims=True) + EPS)


def _norm_kernel(x_ref, g_ref, o_ref):
    o_ref[...] = (_rms(x_ref[...]) * g_ref[...]).astype(o_ref.dtype)


def _norm_call(x2, g):
    T, D = x2.shape
    tm = _pick(T, 512)
    return pl.pallas_call(
        _norm_kernel,
        grid=(T // tm,),
        in_specs=[pl.BlockSpec((tm, D), lambda i: (i, 0)),
                  pl.BlockSpec((1, D), lambda i: (0, 0))],
        out_specs=pl.BlockSpec((tm, D), lambda i: (i, 0)),
        out_shape=jax.ShapeDtypeStruct((T, D), MXU_DTYPE),
        compiler_params=_params("parallel"),
        name="norm",
    )(x2, g.reshape(1, D))


def _halo_specs(tm, T, D, nidx):
    r = tm // HALO
    last = T // HALO - 1

    def pad(f):
        return (lambda i: f(i)) if nidx == 1 else (lambda i, j: f(i))

    return [
        pl.BlockSpec((HALO, D), pad(lambda i: (jnp.maximum(i * r - 1, 0), 0))),
        pl.BlockSpec((tm, D), pad(lambda i: (i, 0))),
        pl.BlockSpec((HALO, D), pad(lambda i: (jnp.minimum((i + 1) * r, last), 0))),
    ]


def _fill_hcat(hcat_ref, hp_ref, hm_ref, hn_ref, tm):
    hcat_ref[pl.ds(0, HALO), :] = hp_ref[...]
    hcat_ref[pl.ds(HALO, tm), :] = hm_ref[...]
    hcat_ref[pl.ds(HALO + tm, HALO), :] = hn_ref[...]


def _pool_kernel(hp_ref, hm_ref, hn_ref, w_ref, pw_ref, ps_ref, o_ref, hcat_ref, *, tm, L):
    rows = tm + 2 * HALO
    gw = pw_ref.shape[-1]
    _fill_hcat(hcat_ref, hp_ref, hm_ref, hn_ref, tm)
    p = _dot(hcat_ref[...], w_ref[...])
    p_hi = p.astype(MXU_DTYPE)
    p_lo = (p - p_hi.astype(F32)).astype(MXU_DTYPE)
    t0 = (pl.program_id(0) * tm) % L
    t = t0 + lax.broadcasted_iota(jnp.int32, (tm, rows), 0)
    tp = t0 - HALO + lax.broadcasted_iota(jnp.int32, (tm, rows), 1)
    inside = (tp >= 0) & (tp < L)
    tc = t0 + lax.broadcasted_iota(jnp.int32, (tm, 1), 0)
    for g, w in enumerate(POOL_WINDOWS):
        lo_off, hi_off = w // 2, w - w // 2
        band = jnp.where(inside & (tp >= t - lo_off) & (tp < t + hi_off), 1.0, 0.0).astype(MXU_DTYPE)
        sl = slice(g * gw, (g + 1) * gw)
        wsum = _dot(band, p_hi[:, sl]) + _dot(band, p_lo[:, sl])
        cnt = (jnp.minimum(tc + hi_off, L) - jnp.maximum(tc - lo_off, 0)).astype(F32)
        pooled = wsum / cnt - p[HALO:HALO + tm, sl]
        mixed = _dot(pooled.astype(MXU_DTYPE), pw_ref[g]) * ps_ref[:, sl]
        o_ref[:, sl] = mixed.astype(o_ref.dtype)


def _pool_call(h, w_pool, pool_w, pool_scale, L):
    T, D = h.shape
    Dp = w_pool.shape[1]
    G, gw, _ = pool_w.shape
    tm = _pick(L, 512, HALO)
    return pl.pallas_call(
        functools.partial(_pool_kernel, tm=tm, L=L),
        grid=(T // tm,),
        in_specs=_halo_specs(tm, T, D, 1) + [
            pl.BlockSpec((D, Dp), lambda i: (0, 0)),
            pl.BlockSpec((G, gw, gw), lambda i: (0, 0, 0)),
            pl.BlockSpec((1, Dp), lambda i: (0, 0)),
        ],
        out_specs=pl.BlockSpec((tm, Dp), lambda i: (i, 0)),
        out_shape=jax.ShapeDtypeStruct((T, Dp), MXU_DTYPE),
        scratch_shapes=[pltpu.VMEM((tm + 2 * HALO, D), MXU_DTYPE)],
        compiler_params=_params("parallel"),
        name="pool",
    )(h, h, h, w_pool, pool_w, pool_scale.reshape(1, Dp))


def _hyproj_kernel(hp_ref, hm_ref, hn_ref, w_ref, cw_ref, cb_ref, o_ref, hcat_ref, p_ref, *, tm, L):
    rows = tm + 2 * HALO

    @pl.when(pl.program_id(1) == 0)
    def _():
        _fill_hcat(hcat_ref, hp_ref, hm_ref, hn_ref, tm)

    p = _dot(hcat_ref[...], w_ref[...])
    t0 = (pl.program_id(0) * tm) % L
    tp = t0 - HALO + lax.broadcasted_iota(jnp.int32, (rows, 1), 0)
    p_ref[...] = jnp.where((tp >= 0) & (tp < L), p, 0.0)
    o_ref[...] = (p_ref[pl.ds(HALO - 1, tm), :] * cw_ref[0:1, :]
                  + p_ref[pl.ds(HALO, tm), :] * cw_ref[1:2, :]
                  + p_ref[pl.ds(HALO + 1, tm), :] * cw_ref[2:3, :]
                  + cb_ref[...])


def _hyproj_call(h, w_hy, conv_w, conv_b, L, C):
    T, D = h.shape
    n3 = w_hy.shape[1]
    tn = C if C % LANES == 0 else n3
    tm = _pick(L, 512, HALO)
    return pl.pallas_call(
        functools.partial(_hyproj_kernel, tm=tm, L=L),
        grid=(T // tm, n3 // tn),
        in_specs=_halo_specs(tm, T, D, 2) + [
            pl.BlockSpec((D, tn), lambda i, j: (0, j)),
            pl.BlockSpec((conv_w.shape[0], tn), lambda i, j: (0, j)),
            pl.BlockSpec((1, tn), lambda i, j: (0, j)),
        ],
        out_specs=pl.BlockSpec((tm, tn), lambda i, j: (i, j)),
        out_shape=jax.ShapeDtypeStruct((T, n3), F32),
        scratch_shapes=[pltpu.VMEM((tm + 2 * HALO, D), MXU_DTYPE),
                        pltpu.VMEM((tm + 2 * HALO, tn), F32)],
        compiler_params=_params("parallel", "arbitrary"),
        name="hyproj",
    )(h, h, h, w_hy, conv_w, conv_b.reshape(1, n3))


def _filter_kernel(w1_ref, b1_ref, f1_ref, w2_ref, b2_ref, f2_ref, w3_ref, two_ref, ss_ref,
                   *, L, C, fb, tmf):
    i = pl.program_id(0)
    N = 2 * L
    hp = lax.Precision.HIGHEST
    m = i * tmf + lax.broadcasted_iota(jnp.int32, (tmf, 1), 0)
    tf = jnp.where(m < L, m, N - m).astype(F32)
    tt = tf / (L - 1)
    wt = (2.0 * math.pi) * tf / L
    lane = lax.broadcasted_iota(jnp.int32, (1, LANES), 1)
    bidx = jnp.where(lane <= fb, lane - 1, lane - 1 - fb).astype(F32)
    bands = 1e-4 + bidx * ((fb - 1 - 1e-4) / (fb - 1))
    ang = bands * wt
    z = jnp.where(lane == 0, tt,
                  jnp.where(lane <= fb, jnp.cos(ang),
                            jnp.where(lane <= 2 * fb, -jnp.sin(ang), 0.0)))
    h = jnp.sin(f1_ref[...] * (_dot(z, w1_ref[...], precision=hp) + b1_ref[...]))
    h = jnp.sin(f2_ref[...] * (_dot(h, w2_ref[...], precision=hp) + b2_ref[...]))
    h3 = _dot(h, w3_ref[...], precision=hp)
    max_decay = math.log(DECAY_TARGET) / FAST_DECAY_PCT
    min_decay = math.log(DECAY_TARGET) / SLOW_DECAY_PCT
    cl = lax.broadcasted_iota(jnp.int32, (1, C), 1).astype(F32)
    deltas = jnp.abs(min_decay + cl * ((max_decay - min_decay) / (C - 1)))
    decay = jnp.exp(-tt * deltas)
    isf = jnp.where(m < L, 1.0, 0.0)
    isb = jnp.where((m > L) | (m == 0), 1.0, 0.0)

    @pl.when(i == 0)
    def _():
        ss_ref[...] = jnp.zeros_like(ss_ref)

    for o in range(2):
        fwd = h3[:, (2 * o) * C:(2 * o + 1) * C]
        bwd = h3[:, (2 * o + 1) * C:(2 * o + 2) * C]
        two = decay * (isf * fwd + isb * bwd)
        two_ref[o] = two
        ss_ref[o] = ss_ref[o] + jnp.sum(two * two, axis=0, keepdims=True)


def _filter_call(L, C, w1, b1, f1, w2, b2, f2, w3):
    N = 2 * L
    fe, fh = w1.shape
    fb = (fe - 1) // 2
    assert fe <= LANES
    w1p = jnp.zeros((LANES, fh), F32).at[:fe].set(w1)
    tmf = _pick(N, 512)
    full = lambda a: pl.BlockSpec(a.shape, lambda i: (0,) * a.ndim)
    args = (w1p, b1.reshape(1, fh), f1.reshape(1, fh), w2, b2.reshape(1, fh), f2.reshape(1, fh), w3)
    return pl.pallas_call(
        functools.partial(_filter_kernel, L=L, C=C, fb=fb, tmf=tmf),
        grid=(N // tmf,),
        in_specs=[full(a) for a in args],
        out_specs=[pl.BlockSpec((2, tmf, C), lambda i: (0, i, 0)),
                   pl.BlockSpec((2, 1, C), lambda i: (0, 0, 0))],
        out_shape=[jax.ShapeDtypeStruct((2, N, C), F32),
                   jax.ShapeDtypeStruct((2, 1, C), F32)],
        compiler_params=_params("arbitrary"),
        name="filter",
    )(*args)


def _stack(mr, mi):
    return jnp.concatenate([jnp.concatenate([mr, -mi], axis=-1),
                            jnp.concatenate([mi, mr], axis=-1)], axis=-2)


def _fft_tables(nc):
    N = nc * NF
    na = nc // 2
    i32 = jnp.int32
    k1 = jnp.arange(nc, dtype=i32)
    ang = (-2.0 * math.pi / nc) * ((k1[:, None] * k1[None, :]) % nc).astype(F32)
    fr, fi = jnp.cos(ang), jnp.sin(ang)
    f_data = _stack(fr[:, :na], fi[:, :na])
    f_filt = jnp.concatenate([fr, fi], axis=0)
    c_inv = _stack(fr.T[:na] / nc, -fi.T[:na] / nc)
    b = jnp.arange(NF, dtype=i32)
    e = (b[None, :, None] * b[None, None, :] * nc + k1[:, None, None] * b[None, None, :]) % N
    ang = (-2.0 * math.pi / N) * e.astype(F32)
    gr, gi = jnp.cos(ang), jnp.sin(ang)
    g_fwd = _stack(gr, gi)
    g_inv = _stack(jnp.swapaxes(gr, 1, 2) / NF, -jnp.swapaxes(gi, 1, 2) / NF)
    cast = lambda t: t.astype(MXU_DTYPE)
    return cast(f_data), cast(f_filt), cast(g_fwd), cast(g_inv), cast(c_inv)


def _rows2d(ref):
    return ref.reshape(math.prod(ref.shape[:-1]), ref.shape[-1])


def _ld_rows(ref, start, n, stride):
    return _rows2d(ref)[pl.ds(start, n, stride=stride), :]


def _st_rows(ref, start, n, stride, val):
    _rows2d(ref)[pl.ds(start, n, stride=stride), :] = val


def _fft_a_kernel(z_ref, f_ref, yr_ref, yi_ref, *, planes, na, bb, nc):
    for b in range(bb):
        x = jnp.concatenate([_ld_rows(z_ref, p * na * bb + b, na, bb) for p in range(planes)], axis=0)
        r = _dot(f_ref[...], x.astype(MXU_DTYPE))
        _st_rows(yr_ref, b, nc, bb, r[:nc])
        _st_rows(yi_ref, b, nc, bb, r[nc:])


def _fft_a_call(z5, col_off, C, fmat, planes, na, nc):
    P = z5.shape[0]
    bb = 16
    off = col_off // LANES
    out = jax.ShapeDtypeStruct((P, nc, NF, C), F32)
    return pl.pallas_call(
        functools.partial(_fft_a_kernel, planes=planes, na=na, bb=bb, nc=nc),
        grid=(P, NF // bb, C // LANES),
        in_specs=[pl.BlockSpec((None, planes, na, bb, LANES), lambda p, jb, jc: (p, 0, 0, jb, jc + off)),
                  pl.BlockSpec(fmat.shape, lambda p, jb, jc: (0, 0))],
        out_specs=[pl.BlockSpec((None, nc, bb, LANES), lambda p, jb, jc: (p, 0, jb, jc))] * 2,
        out_shape=[out, out],
        compiler_params=_params("parallel", "parallel", "parallel"),
        name="fft_a",
    )(z5, fmat)


def _fft_b_spec_kernel(yr_ref, yi_ref, g_ref, ss_ref, or_ref, oi_ref, *, kk):
    s = lax.rsqrt(ss_ref[...] + EPS)
    for k in range(kk):
        x = jnp.concatenate([yr_ref[k], yi_ref[k]], axis=0).astype(MXU_DTYPE)
        z = _dot(g_ref[k], x)
        or_ref[k] = z[:NF] * s
        oi_ref[k] = z[NF:] * s


def _fft_b_conv_kernel(yr_ref, yi_ref, g_ref, hr_ref, hi_ref, gi_ref, or_ref, oi_ref, *, kk):
    for k in range(kk):
        x = jnp.concatenate([yr_ref[k], yi_ref[k]], axis=0).astype(MXU_DTYPE)
        z = _dot(g_ref[k], x)
        zr, zi = z[:NF], z[NF:]
        hr, hi = hr_ref[k], hi_ref[k]
        x2 = jnp.concatenate([zr * hr - zi * hi, zr * hi + zi * hr], axis=0).astype(MXU_DTYPE)
        u = _dot(gi_ref[k], x2)
        or_ref[k] = u[:NF]
        oi_ref[k] = u[NF:]


def _fft_b_call(yr, yi, g_fwd, *, ss=None, spec=None, order=None, g_inv=None):
    P, nc, _, C = yr.shape
    kk = _pick(nc, 8, 1)
    ct = 2 * LANES if C % (2 * LANES) == 0 else C
    yspec = pl.BlockSpec((None, kk, NF, ct), lambda jk, p, jc: (p, jk, 0, jc))
    gspec = pl.BlockSpec((kk, 2 * NF, 2 * NF), lambda jk, p, jc: (jk, 0, 0))
    out = jax.ShapeDtypeStruct((P, nc, NF, C), F32)
    common = dict(grid=(nc // kk, P, C // ct), out_specs=[yspec, yspec], out_shape=[out, out],
                  compiler_params=_params("parallel", "parallel", "parallel"))
    if ss is not None:
        return pl.pallas_call(
            functools.partial(_fft_b_spec_kernel, kk=kk),
            in_specs=[yspec, yspec, gspec, pl.BlockSpec((None, 1, ct), lambda jk, p, jc: (p, 0, jc))],
            name="fft_b_spec", **common,
        )(yr, yi, g_fwd, ss)
    hspec = pl.BlockSpec((None, kk, NF, ct), lambda jk, p, jc: (order, jk, 0, jc))
    return pl.pallas_call(
        functools.partial(_fft_b_conv_kernel, kk=kk),
        in_specs=[yspec, yspec, gspec, hspec, hspec, gspec],
        name="fft_b_conv", **common,
    )(yr, yi, g_fwd, spec[0], spec[1], g_inv)


def _fft_c_kernel(ur_ref, ui_ref, c_ref, gate_ref, zp_ref, bias_ref, o_ref, y_ref, *, na, bb, nc):
    for b in range(bb):
        x = jnp.concatenate([_ld_rows(ur_ref, b, nc, bb), _ld_rows(ui_ref, b, nc, bb)], axis=0)
        _st_rows(y_ref, b, 2 * na, bb, _dot(c_ref[...], x.astype(MXU_DTYPE)))
    o_ref[...] = (gate_ref[...] * (y_ref[...] + bias_ref[...] * zp_ref[...])).astype(o_ref.dtype)


def _fft_c_call(ur, ui, c_inv, gate5, gate_off, zp5, zp_off, bias, na, out_dtype):
    P, nc, _, C = ur.shape
    bb = 16
    goff, zoff = gate_off // LANES, zp_off // LANES
    uspec = pl.BlockSpec((None, nc, bb, LANES), lambda p, jb, jc: (p, 0, jb, jc))
    return pl.pallas_call(
        functools.partial(_fft_c_kernel, na=na, bb=bb, nc=nc),
        grid=(P, NF // bb, C // LANES),
        in_specs=[uspec, uspec,
                  pl.BlockSpec(c_inv.shape, lambda p, jb, jc: (0, 0)),
                  pl.BlockSpec((None, 2, na, bb, LANES), lambda p, jb, jc: (p, 0, 0, jb, jc + goff)),
                  pl.BlockSpec((None, 2, na, bb, LANES), lambda p, jb, jc: (p, 0, 0, jb, jc + zoff)),
                  pl.BlockSpec((1, LANES), lambda p, jb, jc: (0, jc))],
        out_specs=pl.BlockSpec((None, 2, na, bb, LANES), lambda p, jb, jc: (p, 0, 0, jb, jc)),
        out_shape=jax.ShapeDtypeStruct((P, 2, na, NF, C), out_dtype),
        scratch_shapes=[pltpu.VMEM((2, na, bb, LANES), F32)],
        compiler_params=_params("parallel", "parallel", "parallel"),
        name="fft_c",
    )(ur, ui, c_inv, gate5, zp5, bias.reshape(1, C))


def _merge_kernel(h_ref, a_ref, b_ref, x_ref, wga_ref, wgb_ref, wa_ref, wb_ref, wo_ref, o_ref):
    h = h_ref[...]
    ga = jax.nn.sigmoid(_dot(h, wga_ref[...]))
    gb = jax.nn.sigmoid(_dot(h, wgb_ref[...]))
    m = ga * _dot(a_ref[...], wa_ref[...]) + gb * _dot(b_ref[...], wb_ref[...])
    contrib = _dot(m.astype(MXU_DTYPE), wo_ref[...])

    @pl.when(pl.program_id(1) == 0)
    def _():
        o_ref[...] = x_ref[...] + contrib

    @pl.when(pl.program_id(1) != 0)
    def _():
        o_ref[...] += contrib


def _merge_call(h, a, b, x2, w_ga, w_gb, wa, wb, wo):
    T, D = x2.shape
    Dp, C = a.shape[1], b.shape[1]
    tm = _pick(T, 512)
    tn = _pick(D, 512, LANES)
    return pl.pallas_call(
        _merge_kernel,
        grid=(T // tm, D // tn),
        in_specs=[pl.BlockSpec((tm, D), lambda i, j: (i, 0)),
                  pl.BlockSpec((tm, Dp), lambda i, j: (i, 0)),
                  pl.BlockSpec((tm, C), lambda i, j: (i, 0)),
                  pl.BlockSpec((tm, D), lambda i, j: (i, 0)),
                  pl.BlockSpec((D, tn), lambda i, j: (0, j)),
                  pl.BlockSpec((D, tn), lambda i, j: (0, j)),
                  pl.BlockSpec((Dp, tn), lambda i, j: (0, j)),
                  pl.BlockSpec((C, tn), lambda i, j: (0, j)),
                  pl.BlockSpec((tn, D), lambda i, j: (j, 0))],
        out_specs=pl.BlockSpec((tm, D), lambda i, j: (i, 0)),
        out_shape=jax.ShapeDtypeStruct((T, D), F32),
        compiler_params=_params("parallel", "arbitrary"),
        name="merge",
    )(h, a, b, x2, w_ga, w_gb, wa, wb, wo)


def _ffn_kernel(x_ref, g_ref, wg_ref, wu_ref, wd_ref, gf_ref, o_ref, h_ref, acc_ref):
    j = pl.program_id(1)

    @pl.when(j == 0)
    def _():
        x = x_ref[...]
        h_ref[...] = (_rms(x) * g_ref[...]).astype(h_ref.dtype)
        acc_ref[...] = x

    h = h_ref[...]
    act = jax.nn.silu(_dot(h, wg_ref[...])) * _dot(h, wu_ref[...])
    acc_ref[...] += _dot(act.astype(MXU_DTYPE), wd_ref[...])

    @pl.when(j == pl.num_programs(1) - 1)
    def _():
        o_ref[...] = _rms(acc_ref[...]) * gf_ref[...]


def _ffn_call(x2, g, wg, wu, wd, gf):
    T, D = x2.shape
    Dff = wg.shape[1]
    tm = _pick(T, 512)
    tf = _pick(Dff, 512, LANES)
    return pl.pallas_call(
        _ffn_kernel,
        grid=(T // tm, Dff // tf),
        in_specs=[pl.BlockSpec((tm, D), lambda i, j: (i, 0)),
                  pl.BlockSpec((1, D), lambda i, j: (0, 0)),
                  pl.BlockSpec((D, tf), lambda i, j: (0, j)),
                  pl.BlockSpec((D, tf), lambda i, j: (0, j)),
                  pl.BlockSpec((tf, D), lambda i, j: (j, 0)),
                  pl.BlockSpec((1, D), lambda i, j: (0, 0))],
        out_specs=pl.BlockSpec((tm, D), lambda i, j: (i, 0)),
        out_shape=jax.ShapeDtypeStruct((T, D), F32),
        scratch_shapes=[pltpu.VMEM((tm, D), MXU_DTYPE), pltpu.VMEM((tm, D), F32)],
        compiler_params=_params("parallel", "arbitrary"),
        name="ffn",
    )(x2, g.reshape(1, D), wg, wu, wd, gf.reshape(1, D))


def _trunk(x, w):
    B, L, D = x.shape
    T = B * L
    C = w["hyena_bias"].shape[1]
    assert B % 2 == 0 and L % NF == 0 and C % LANES == 0
    na = L // NF
    nc = 2 * na
    P = B // 2
    x2 = x.reshape(T, D)

    h = _norm_call(x2, w["g_mix"])
    a = _pool_call(h, w["w_pool"], w["pool_w"], w["pool_scale"], L)
    uc = _hyproj_call(h, w["w_hy"], w["conv_w"], w["conv_b"], L, C)
    uc5 = uc.reshape(P, 2, na, NF, 3 * C)

    f_data, f_filt, g_fwd, g_inv, c_inv = _fft_tables(nc)
    two, ss = _filter_call(L, C, *w["filt"])
    tr, ti = _fft_a_call(two.reshape(2, 1, nc, NF, C), 0, C, f_filt, 1, nc, nc)
    spec = _fft_b_call(tr, ti, g_fwd, ss=ss)

    z5, z_off = uc5, 0
    for o in range(2):
        yr, yi = _fft_a_call(z5, z_off, C, f_data, 2, na, nc)
        ur, ui = _fft_b_call(yr, yi, g_fwd, spec=spec, order=o, g_inv=g_inv)
        z5 = _fft_c_call(ur, ui, c_inv, uc5, (o + 1) * C, z5, z_off, w["hyena_bias"][o], na,
                         F32 if o == 0 else MXU_DTYPE)
        z_off = 0
    b = z5.reshape(T, C)

    xn = _merge_call(h, a, b, x2, w["w_ga"], w["w_gb"], w["wa"], w["wb"], w["wo"])
    y = _ffn_call(xn, w["g_ffn"], w["wg"], w["wu"], w["wd"], w["g_final"])
    return y.reshape(B, L, D)


def kernel(x_prompt, x_sample, g_mix, w_in, pool_w, pool_scale, conv_w, conv_b, filt_w1, filt_b1, filt_freq1, filt_w2, filt_b2, filt_freq2, filt_w3, hyena_bias, w_branch_a, w_branch_b, w_out, g_ffn, w_gate, w_up, w_down, g_final):
    assert g_mix.shape[0] == 1, "depth-1 block"
    D = x_prompt.shape[-1]
    Dp = pool_scale.shape[1]
    C = hyena_bias.shape[2]
    s1, s2, s3 = Dp, Dp + 3 * C, Dp + 3 * C + D
    cast = lambda t: t.astype(MXU_DTYPE)
    w = {
        "g_mix": g_mix[0], "w_pool": cast(w_in[0][:, :s1]), "w_hy": cast(w_in[0][:, s1:s2]),
        "w_ga": cast(w_in[0][:, s2:s3]), "w_gb": cast(w_in[0][:, s3:]),
        "pool_w": cast(pool_w[0]), "pool_scale": pool_scale[0],
        "conv_w": conv_w[0], "conv_b": conv_b[0],
        "filt": (filt_w1[0], filt_b1[0], filt_freq1[0], filt_w2[0], filt_b2[0], filt_freq2[0], filt_w3[0]),
        "hyena_bias": hyena_bias[0],
        "wa": cast(w_branch_a[0]), "wb": cast(w_branch_b[0]), "wo": cast(w_out[0]),
        "g_ffn": g_ffn[0], "wg": cast(w_gate[0]), "wu": cast(w_up[0]), "wd": cast(w_down[0]),
        "g_final": g_final,
    }
    return (_trunk(x_prompt, w), _trunk(x_sample, w))
```

```python
import functools
import math

import jax
import jax.numpy as jnp
from jax import lax
from jax.experimental import pallas as pl
from jax.experimental.pallas import tpu as pltpu

F32 = jnp.float32
MXU_DTYPE = jnp.bfloat16

EPS = 1e-6
POOL_WINDOWS = (2, 4, 8, 16)
DECAY_TARGET = 1e-2
FAST_DECAY_PCT = 0.3
SLOW_DECAY_PCT = 1.5

LANES = 128
NF = 128
HALO = 16
VMEM_LIMIT = 56 * 1024 * 1024


def _pick(n, pref, mult=8):
    best = None
    for d in range(mult, min(n, pref) + 1, mult):
        if n % d == 0:
            best = d
    return best if best is not None else n


def _params(*sem):
    return pltpu.CompilerParams(dimension_semantics=sem, vmem_limit_bytes=VMEM_LIMIT)


def _dot(a, b, **kw):
    return jnp.dot(a, b, preferred_element_type=F32, **kw)


def _rms(x):
    return x * lax.rsqrt(jnp.mean(x * x, axis=-1, keepdims=True) + EPS)


def _norm_kernel(x_ref, g_ref, o_ref):
    o_ref[...] = (_rms(x_ref[...]) * g_ref[...]).astype(o_ref.dtype)


def _norm_call(x2, g):
    T, D = x2.shape
    tm = _pick(T, 512)
    return pl.pallas_call(
        _norm_kernel,
        grid=(T // tm,),
        in_specs=[pl.BlockSpec((tm, D), lambda i: (i, 0)),
                  pl.BlockSpec((1, D), lambda i: (0, 0))],
        out_specs=pl.BlockSpec((tm, D), lambda i: (i, 0)),
        out_shape=jax.ShapeDtypeStruct((T, D), MXU_DTYPE),
        compiler_params=_params("parallel"),
        name="norm",
    )(x2, g.reshape(1, D))


def _halo_specs(tm, T, D, nidx):
    r = tm // HALO
    last = T // HALO - 1

    def pad(f):
        return (lambda i: f(i)) if nidx == 1 else (lambda i, j: f(i))

    return [
        pl.BlockSpec((HALO, D), pad(lambda i: (jnp.maximum(i * r - 1, 0), 0))),
        pl.BlockSpec((tm, D), pad(lambda i: (i, 0))),
        pl.BlockSpec((HALO, D), pad(lambda i: (jnp.minimum((i + 1) * r, last), 0))),
    ]


def _fill_hcat(hcat_ref, hp_ref, hm_ref, hn_ref, tm):
    hcat_ref[pl.ds(0, HALO), :] = hp_ref[...]
    hcat_ref[pl.ds(HALO, tm), :] = hm_ref[...]
    hcat_ref[pl.ds(HALO + tm, HALO), :] = hn_ref[...]


def _pool_kernel(hp_ref, hm_ref, hn_ref, w_ref, pw_ref, ps_ref, o_ref, hcat_ref, *, tm, L):
    rows = tm + 2 * HALO
    gw = pw_ref.shape[-1]
    _fill_hcat(hcat_ref, hp_ref, hm_ref, hn_ref, tm)
    p = _dot(hcat_ref[...], w_ref[...])
    p_hi = p.astype(MXU_DTYPE)
    p_lo = (p - p_hi.astype(F32)).astype(MXU_DTYPE)
    t0 = (pl.program_id(0) * tm) % L
    t = t0 + lax.broadcasted_iota(jnp.int32, (tm, rows), 0)
    tp = t0 - HALO + lax.broadcasted_iota(jnp.int32, (tm, rows), 1)
    inside = (tp >= 0) & (tp < L)
    tc = t0 + lax.broadcasted_iota(jnp.int32, (tm, 1), 0)
    for g, w in enumerate(POOL_WINDOWS):
        lo_off, hi_off = w // 2, w - w // 2
        band = jnp.where(inside & (tp >= t - lo_off) & (tp < t + hi_off), 1.0, 0.0).astype(MXU_DTYPE)
        sl = slice(g * gw, (g + 1) * gw)
        wsum = _dot(band, p_hi[:, sl]) + _dot(band, p_lo[:, sl])
        cnt = (jnp.minimum(tc + hi_off, L) - jnp.maximum(tc - lo_off, 0)).astype(F32)
        pooled = wsum / cnt - p[HALO:HALO + tm, sl]
        mixed = _dot(pooled.astype(MXU_DTYPE), pw_ref[g]) * ps_ref[:, sl]
        o_ref[:, sl] = mixed.astype(o_ref.dtype)


def _pool_call(h, w_pool, pool_w, pool_scale, L):
    T, D = h.shape
    Dp = w_pool.shape[1]
    G, gw, _ = pool_w.shape
    tm = _pick(L, 512, HALO)
    return pl.pallas_call(
        functools.partial(_pool_kernel, tm=tm, L=L),
        grid=(T // tm,),
        in_specs=_halo_specs(tm, T, D, 1) + [
            pl.BlockSpec((D, Dp), lambda i: (0, 0)),
            pl.BlockSpec((G, gw, gw), lambda i: (0, 0, 0)),
            pl.BlockSpec((1, Dp), lambda i: (0, 0)),
        ],
        out_specs=pl.BlockSpec((tm, Dp), lambda i: (i, 0)),
        out_shape=jax.ShapeDtypeStruct((T, Dp), MXU_DTYPE),
        scratch_shapes=[pltpu.VMEM((tm + 2 * HALO, D), MXU_DTYPE)],
        compiler_params=_params("parallel"),
        name="pool",
    )(h, h, h, w_pool, pool_w, pool_scale.reshape(1, Dp))


def _hyproj_kernel(hp_ref, hm_ref, hn_ref, w_ref, cw_ref, cb_ref, o_ref, hcat_ref, p_ref, *, tm, L):
    rows = tm + 2 * HALO

    @pl.when(pl.program_id(1) == 0)
    def _():
        _fill_hcat(hcat_ref, hp_ref, hm_ref, hn_ref, tm)

    p = _dot(hcat_ref[...], w_ref[...])
    t0 = (pl.program_id(0) * tm) % L
    tp = t0 - HALO + lax.broadcasted_iota(jnp.int32, (rows, 1), 0)
    p_ref[...] = jnp.where((tp >= 0) & (tp < L), p, 0.0)
    o_ref[...] = (p_ref[pl.ds(HALO - 1, tm), :] * cw_ref[0:1, :]
                  + p_ref[pl.ds(HALO, tm), :] * cw_ref[1:2, :]
                  + p_ref[pl.ds(HALO + 1, tm), :] * cw_ref[2:3, :]
                  + cb_ref[...])


def _hyproj_call(h, w_hy, conv_w, conv_b, L, C):
    T, D = h.shape
    n3 = w_hy.shape[1]
    tn = C if C % LANES == 0 else n3
    tm = _pick(L, 512, HALO)
    return pl.pallas_call(
        functools.partial(_hyproj_kernel, tm=tm, L=L),
        grid=(T // tm, n3 // tn),
        in_specs=_halo_specs(tm, T, D, 2) + [
            pl.BlockSpec((D, tn), lambda i, j: (0, j)),
            pl.BlockSpec((conv_w.shape[0], tn), lambda i, j: (0, j)),
            pl.BlockSpec((1, tn), lambda i, j: (0, j)),
        ],
        out_specs=pl.BlockSpec((tm, tn), lambda i, j: (i, j)),
        out_shape=jax.ShapeDtypeStruct((T, n3), F32),
        scratch_shapes=[pltpu.VMEM((tm + 2 * HALO, D), MXU_DTYPE),
                        pltpu.VMEM((tm + 2 * HALO, tn), F32)],
        compiler_params=_params("parallel", "arbitrary"),
        name="hyproj",
    )(h, h, h, w_hy, conv_w, conv_b.reshape(1, n3))


def _filter_kernel(w1_ref, b1_ref, f1_ref, w2_ref, b2_ref, f2_ref, w3_ref, two_ref, ss_ref,
                   *, L, C, fb, tmf):
    i = pl.program_id(0)
    N = 2 * L
    hp = lax.Precision.HIGHEST
    m = i * tmf + lax.broadcasted_iota(jnp.int32, (tmf, 1), 0)
    tf = jnp.where(m < L, m, N - m).astype(F32)
    tt = tf / (L - 1)
    wt = (2.0 * math.pi) * tf / L
    lane = lax.broadcasted_iota(jnp.int32, (1, LANES), 1)
    bidx = jnp.where(lane <= fb, lane - 1, lane - 1 - fb).astype(F32)
    bands = 1e-4 + bidx * ((fb - 1 - 1e-4) / (fb - 1))
    ang = bands * wt
    z = jnp.where(lane == 0, tt,
                  jnp.where(lane <= fb, jnp.cos(ang),
                            jnp.where(lane <= 2 * fb, -jnp.sin(ang), 0.0)))
    h = jnp.sin(f1_ref[...] * (_dot(z, w1_ref[...], precision=hp) + b1_ref[...]))
    h = jnp.sin(f2_ref[...] * (_dot(h, w2_ref[...], precision=hp) + b2_ref[...]))
    h_hi = h.astype(MXU_DTYPE)
    h_lo = (h - h_hi.astype(F32)).astype(MXU_DTYPE)
    hk = jnp.concatenate([h_hi, h_hi, h_lo], axis=1)
    max_decay = math.log(DECAY_TARGET) / FAST_DECAY_PCT
    min_decay = math.log(DECAY_TARGET) / SLOW_DECAY_PCT
    cl = lax.broadcasted_iota(jnp.int32, (1, C), 1).astype(F32)
    deltas = jnp.abs(min_decay + cl * ((max_decay - min_decay) / (C - 1)))
    decay = jnp.where(m == L, 0.0, jnp.exp(-tt * deltas))
    decay2 = jnp.concatenate([decay, decay], axis=1)

    def emit(two):
        two_ref[0] = two[:, :C]
        two_ref[1] = two[:, C:]
        ss_ref[...] += jnp.sum(two * two, axis=0, keepdims=True)

    @pl.when(i == 0)
    def _():
        ss_ref[...] = jnp.zeros_like(ss_ref)
        lag0 = jnp.where(m == 0, 1.0, 0.0)
        emit(decay2 * (_dot(hk, w3_ref[0]) + lag0 * _dot(hk, w3_ref[1])))

    @pl.when(i != 0)
    def _():
        emit(decay2 * _dot(hk, w3_ref[(i * tmf >= L).astype(jnp.int32)]))


def _filter_call(L, C, w1, b1, f1, w2, b2, f2, w3):
    N = 2 * L
    fe, fh = w1.shape
    fb = (fe - 1) // 2
    assert fe <= LANES
    w1p = jnp.zeros((LANES, fh), F32).at[:fe].set(w1)
    tmf = _pick(L, 512)
    w3d = jnp.transpose(w3.reshape(fh, 2, 2, C), (2, 0, 1, 3)).reshape(2, fh, 2 * C)
    w3_hi = w3d.astype(MXU_DTYPE)
    w3_lo = (w3d - w3_hi.astype(F32)).astype(MXU_DTYPE)
    w3k = jnp.concatenate([w3_hi, w3_lo, w3_hi], axis=1)
    full = lambda a: pl.BlockSpec(a.shape, lambda i: (0,) * a.ndim)
    args = (w1p, b1.reshape(1, fh), f1.reshape(1, fh), w2, b2.reshape(1, fh), f2.reshape(1, fh), w3k)
    two, ss = pl.pallas_call(
        functools.partial(_filter_kernel, L=L, C=C, fb=fb, tmf=tmf),
        grid=(N // tmf,),
        in_specs=[full(a) for a in args],
        out_specs=[pl.BlockSpec((2, tmf, C), lambda i: (0, i, 0)),
                   pl.BlockSpec((1, 2 * C), lambda i: (0, 0))],
        out_shape=[jax.ShapeDtypeStruct((2, N, C), F32),
                   jax.ShapeDtypeStruct((1, 2 * C), F32)],
        compiler_params=_params("arbitrary"),
        name="filter",
    )(*args)
    return two, ss.reshape(2, 1, C)


def _stack(mr, mi):
    return jnp.concatenate([jnp.concatenate([mr, -mi], axis=-1),
                            jnp.concatenate([mi, mr], axis=-1)], axis=-2)


def _fft_tables(nc):
    N = nc * NF
    na = nc // 2
    i32 = jnp.int32
    k1 = jnp.arange(nc, dtype=i32)
    b = jnp.arange(NF, dtype=i32)
    e = (k1[None, :, None] * k1[None, None, :] * NF + b[:, None, None] * k1[None, :, None]) % N
    ang = (-2.0 * math.pi / N) * e.astype(F32)
    ar, ai = jnp.cos(ang), jnp.sin(ang)
    a_data = _stack(ar[:, :, :na], ai[:, :, :na])
    a_filt = jnp.concatenate([ar, ai], axis=1)
    cr, ci = jnp.swapaxes(ar, 1, 2)[:, :na] / nc, -jnp.swapaxes(ai, 1, 2)[:, :na] / nc
    c_inv = _stack(cr, ci)
    ang = (-2.0 * math.pi / NF) * ((b[:, None] * b[None, :]) % NF).astype(F32)
    gr, gi = jnp.cos(ang), jnp.sin(ang)
    g_fwd = _stack(gr, gi)
    g_inv = _stack(gr.T / NF, -gi.T / NF)
    cast = lambda t: t.astype(MXU_DTYPE)
    return cast(a_data), cast(a_filt), cast(g_fwd), cast(g_inv), cast(c_inv)


BB = 16
NFP = NF + 8
YCP = BB + 8
U32 = jnp.uint32
PACKED = True


def _rows2d(ref):
    return ref.reshape(math.prod(ref.shape[:-1]), ref.shape[-1])


def _ld_rows(ref, start, n, stride):
    return _rows2d(ref)[pl.ds(start, n, stride=stride), :]


def _st_rows(ref, start, n, stride, val):
    _rows2d(ref)[pl.ds(start, n, stride=stride), :] = val


def _spec_scratch(nc):
    if PACKED:
        return pltpu.VMEM((2, 1, nc * NFP, LANES), U32)
    return pltpu.VMEM((2, 2, nc * NFP, LANES), F32)


def _spec_store(y_ref, rows, re, im):
    if PACKED:
        rb = lax.bitcast_convert_type(re.astype(jnp.bfloat16).astype(F32), U32)
        ib = lax.bitcast_convert_type(im.astype(jnp.bfloat16).astype(F32), U32)
        w = rb | (ib >> 16)
        y_ref[0, 0, rows, :] = w[:, :LANES]
        y_ref[1, 0, rows, :] = w[:, LANES:]
    else:
        for half in range(2):
            y_ref[half, 0, rows, :] = re[:, half * LANES:(half + 1) * LANES]
            y_ref[half, 1, rows, :] = im[:, half * LANES:(half + 1) * LANES]


def _spec_load(y_ref, rows):
    if PACKED:
        w = jnp.concatenate([y_ref[0, 0, rows, :], y_ref[1, 0, rows, :]], axis=1)
        re = lax.bitcast_convert_type(w & U32(0xFFFF0000), F32)
        im = lax.bitcast_convert_type(w << 16, F32)
    else:
        re = jnp.concatenate([y_ref[0, 0, rows, :], y_ref[1, 0, rows, :]], axis=1)
        im = jnp.concatenate([y_ref[0, 1, rows, :], y_ref[1, 1, rows, :]], axis=1)
    return jnp.concatenate([re, im], axis=0).astype(MXU_DTYPE)


def _phase_a(z_refs, at_ref, y_ref, step, *, planes, na, nc):
    for b in range(BB):
        x = jnp.concatenate(
            [jnp.concatenate([_ld_rows(z, p * na * BB + b, na, BB) for p in range(planes)], axis=0)
             for z in z_refs], axis=1)
        r = _dot(at_ref[b], x.astype(MXU_DTYPE))
        _spec_store(y_ref, pl.ds(step * BB + b, nc, stride=NFP), r[:nc], r[nc:])


def _k1_rows(step, kk, k):
    return pl.ds(pl.multiple_of((step * kk + k) * NFP, 8), NF)


def _lconv_kernel(z0_ref, z1_ref, gate_ref, at_ref, ct_ref, gf_ref, gi_ref, hr_ref, hi_ref,
                  o_ref, y_ref, yc_ref, *, na, nc, kk, sa, sb):
    s = pl.program_id(2)

    @pl.when(s < sa)
    def _():
        _phase_a((z0_ref, z1_ref), at_ref, y_ref, s, planes=2, na=na, nc=nc)

    @pl.when((s >= sa) & (s < sa + sb))
    def _():
        for k in range(kk):
            rows = _k1_rows(s - sa, kk, k)
            z = _dot(gf_ref[...], _spec_load(y_ref, rows))
            zr, zi = z[:NF], z[NF:]
            hr, hi = hr_ref[k], hi_ref[k]
            x2 = jnp.concatenate([zr * hr - zi * hi, zr * hi + zi * hr], axis=0).astype(MXU_DTYPE)
            u = _dot(gi_ref[...], x2)
            _spec_store(y_ref, rows, u[:NF], u[NF:])

    @pl.when(s >= sa + sb)
    def _():
        step = s - sa - sb
        for b in range(BB):
            x = _spec_load(y_ref, pl.ds(step * BB + b, nc, stride=NFP))
            r = _dot(ct_ref[b], x)
            for half in range(2):
                _st_rows(yc_ref.at[half], b, 2 * na, YCP, r[:, half * LANES:(half + 1) * LANES])
        for half in range(2):
            lanes = slice(half * LANES, (half + 1) * LANES)
            y = yc_ref[half, :, :BB, :].reshape(2, na, BB, LANES)
            o_ref[:, :, :, lanes] = (gate_ref[:, :, :, lanes] * y).astype(o_ref.dtype)


def _lconv_call(z5, z_off, gate5, gate_off, spec, order, tabs, na, out_dtype):
    a_data, _, g_fwd, g_inv, c_inv = tabs
    P = z5.shape[0]
    C = spec[0].shape[-1]
    nc = 2 * na
    ct = 2 * LANES
    kk = _pick(nc, 16, 1)
    sa, sb = NF // BB, nc // kk
    zo, go = z_off // LANES, gate_off // ct
    ia = lambda s: jnp.minimum(s, sa - 1)
    ib = lambda s: jnp.clip(s - sa, 0, sb - 1)
    ic = lambda s: jnp.maximum(s - sa - sb, 0)
    zspec = lambda h: pl.BlockSpec((None, 2, na, BB, LANES), lambda p, jc, s: (p, 0, 0, ia(s), zo + 2 * jc + h))
    hspec = pl.BlockSpec((None, kk, NF, ct), lambda p, jc, s: (order, ib(s), 0, jc))
    gspec = pl.BlockSpec((2 * NF, 2 * NF), lambda p, jc, s: (0, 0))
    return pl.pallas_call(
        functools.partial(_lconv_kernel, na=na, nc=nc, kk=kk, sa=sa, sb=sb),
        grid=(P, C // ct, 2 * sa + sb),
        in_specs=[zspec(0), zspec(1),
                  pl.BlockSpec((None, 2, na, BB, ct), lambda p, jc, s: (p, 0, 0, ic(s), go + jc)),
                  pl.BlockSpec((BB, 2 * nc, nc), lambda p, jc, s: (ia(s), 0, 0)),
                  pl.BlockSpec((BB, nc, 2 * nc), lambda p, jc, s: (ic(s), 0, 0)),
                  gspec, gspec, hspec, hspec],
        out_specs=pl.BlockSpec((None, 2, na, BB, ct), lambda p, jc, s: (p, 0, 0, ic(s), jc)),
        out_shape=jax.ShapeDtypeStruct((P, 2, na, NF, C), out_dtype),
        scratch_shapes=[_spec_scratch(nc), pltpu.VMEM((2, 2 * na, YCP, LANES), F32)],
        compiler_params=_params("parallel", "parallel", "arbitrary"),
        name="lconv",
    )(z5, z5, gate5, a_data, c_inv, g_fwd, g_inv, spec[0], spec[1])


def _fspec_kernel(t0_ref, t1_ref, at_ref, gf_ref, ss_ref, bias_ref, hr_ref, hi_ref, y_ref,
                  *, nc, kk, sa):
    s = pl.program_id(2)

    @pl.when(s < sa)
    def _():
        _phase_a((t0_ref, t1_ref), at_ref, y_ref, s, planes=1, na=nc, nc=nc)

    @pl.when(s >= sa)
    def _():
        scale = lax.rsqrt(ss_ref[...] + EPS)
        for k in range(kk):
            z = _dot(gf_ref[...], _spec_load(y_ref, _k1_rows(s - sa, kk, k)))
            hr_ref[k] = z[:NF] * scale + bias_ref[...]
            hi_ref[k] = z[NF:] * scale


def _fspec_call(two, ss, bias, tabs, nc):
    _, a_filt, g_fwd, _, _ = tabs
    C = two.shape[-1]
    ct = 2 * LANES
    kk = _pick(nc, 16, 1)
    sa, sb = NF // BB, nc // kk
    two5 = two.reshape(2, 1, nc, NF, C)
    ia = lambda s: jnp.minimum(s, sa - 1)
    ib = lambda s: jnp.maximum(s - sa, 0)
    tspec = lambda h: pl.BlockSpec((None, 1, nc, BB, LANES), lambda o, jc, s: (o, 0, 0, ia(s), 2 * jc + h))
    vspec = pl.BlockSpec((None, 1, ct), lambda o, jc, s: (o, 0, jc))
    hspec = pl.BlockSpec((None, kk, NF, ct), lambda o, jc, s: (o, ib(s), 0, jc))
    out = jax.ShapeDtypeStruct((2, nc, NF, C), F32)
    return pl.pallas_call(
        functools.partial(_fspec_kernel, nc=nc, kk=kk, sa=sa),
        grid=(2, C // ct, sa + sb),
        in_specs=[tspec(0), tspec(1),
                  pl.BlockSpec((BB, 2 * nc, nc), lambda o, jc, s: (ia(s), 0, 0)),
                  pl.BlockSpec((2 * NF, 2 * NF), lambda o, jc, s: (0, 0)),
                  vspec, vspec],
        out_specs=[hspec, hspec],
        out_shape=[out, out],
        scratch_shapes=[_spec_scratch(nc)],
        compiler_params=_params("parallel", "parallel", "arbitrary"),
        name="fspec",
    )(two5, two5, a_filt, g_fwd, ss, bias)


def _merge_kernel(h_ref, a_ref, b_ref, x_ref, wga_ref, wgb_ref, wa_ref, wb_ref, wo_ref, o_ref):
    h = h_ref[...]
    ga = jax.nn.sigmoid(_dot(h, wga_ref[...]))
    gb = jax.nn.sigmoid(_dot(h, wgb_ref[...]))
    m = ga * _dot(a_ref[...], wa_ref[...]) + gb * _dot(b_ref[...], wb_ref[...])
    contrib = _dot(m.astype(MXU_DTYPE), wo_ref[...])

    @pl.when(pl.program_id(1) == 0)
    def _():
        o_ref[...] = x_ref[...] + contrib

    @pl.when(pl.program_id(1) != 0)
    def _():
        o_ref[...] += contrib


def _merge_call(h, a, b, x2, w_ga, w_gb, wa, wb, wo):
    T, D = x2.shape
    Dp, C = a.shape[1], b.shape[1]
    tm = _pick(T, 512)
    tn = _pick(D, 512, LANES)
    return pl.pallas_call(
        _merge_kernel,
        grid=(T // tm, D // tn),
        in_specs=[pl.BlockSpec((tm, D), lambda i, j: (i, 0)),
                  pl.BlockSpec((tm, Dp), lambda i, j: (i, 0)),
                  pl.BlockSpec((tm, C), lambda i, j: (i, 0)),
                  pl.BlockSpec((tm, D), lambda i, j: (i, 0)),
                  pl.BlockSpec((D, tn), lambda i, j: (0, j)),
                  pl.BlockSpec((D, tn), lambda i, j: (0, j)),
                  pl.BlockSpec((Dp, tn), lambda i, j: (0, j)),
                  pl.BlockSpec((C, tn), lambda i, j: (0, j)),
                  pl.BlockSpec((tn, D), lambda i, j: (j, 0))],
        out_specs=pl.BlockSpec((tm, D), lambda i, j: (i, 0)),
        out_shape=jax.ShapeDtypeStruct((T, D), F32),
        compiler_params=_params("parallel", "arbitrary"),
        name="merge",
    )(h, a, b, x2, w_ga, w_gb, wa, wb, wo)


def _ffn_kernel(x_ref, g_ref, wg_ref, wu_ref, wd_ref, gf_ref, o_ref, h_ref, acc_ref):
    j = pl.program_id(1)

    @pl.when(j == 0)
    def _():
        x = x_ref[...]
        h_ref[...] = (_rms(x) * g_ref[...]).astype(h_ref.dtype)
        acc_ref[...] = x

    h = h_ref[...]
    act = jax.nn.silu(_dot(h, wg_ref[...])) * _dot(h, wu_ref[...])
    acc_ref[...] += _dot(act.astype(MXU_DTYPE), wd_ref[...])

    @pl.when(j == pl.num_programs(1) - 1)
    def _():
        o_ref[...] = _rms(acc_ref[...]) * gf_ref[...]


def _ffn_call(x2, g, wg, wu, wd, gf):
    T, D = x2.shape
    Dff = wg.shape[1]
    tm = _pick(T, 512)
    tf = _pick(Dff, 512, LANES)
    return pl.pallas_call(
        _ffn_kernel,
        grid=(T // tm, Dff // tf),
        in_specs=[pl.BlockSpec((tm, D), lambda i, j: (i, 0)),
                  pl.BlockSpec((1, D), lambda i, j: (0, 0)),
                  pl.BlockSpec((D, tf), lambda i, j: (0, j)),
                  pl.BlockSpec((D, tf), lambda i, j: (0, j)),
                  pl.BlockSpec((tf, D), lambda i, j: (j, 0)),
                  pl.BlockSpec((1, D), lambda i, j: (0, 0))],
        out_specs=pl.BlockSpec((tm, D), lambda i, j: (i, 0)),
        out_shape=jax.ShapeDtypeStruct((T, D), F32),
        scratch_shapes=[pltpu.VMEM((tm, D), MXU_DTYPE), pltpu.VMEM((tm, D), F32)],
        compiler_params=_params("parallel", "arbitrary"),
        name="ffn",
    )(x2, g.reshape(1, D), wg, wu, wd, gf.reshape(1, D))


def _trunk(x, w):
    B, L, D = x.shape
    T = B * L
    C = w["hyena_bias"].shape[1]
    assert B % 2 == 0 and L % NF == 0 and C % (2 * LANES) == 0 and w["hyena_bias"].shape[0] == 2
    na = L // NF
    nc = 2 * na
    P = B // 2
    x2 = x.reshape(T, D)

    h = _norm_call(x2, w["g_mix"])
    a = _pool_call(h, w["w_pool"], w["pool_w"], w["pool_scale"], L)
    uc = _hyproj_call(h, w["w_hy"], w["conv_w"], w["conv_b"], L, C)
    uc5 = uc.reshape(P, 2, na, NF, 3 * C)

    tabs = _fft_tables(nc)
    two, ss = _filter_call(L, C, *w["filt"])
    spec = _fspec_call(two, ss, w["hyena_bias"].reshape(2, 1, C), tabs, nc)

    z1 = _lconv_call(uc5, 0, uc5, C, spec, 0, tabs, na, F32)
    b = _lconv_call(z1, 0, uc5, 2 * C, spec, 1, tabs, na, MXU_DTYPE).reshape(T, C)

    xn = _merge_call(h, a, b, x2, w["w_ga"], w["w_gb"], w["wa"], w["wb"], w["wo"])
    y = _ffn_call(xn, w["g_ffn"], w["wg"], w["wu"], w["wd"], w["g_final"])
    return y.reshape(B, L, D)


def kernel(x_prompt, x_sample, g_mix, w_in, pool_w, pool_scale, conv_w, conv_b, filt_w1, filt_b1, filt_freq1, filt_w2, filt_b2, filt_freq2, filt_w3, hyena_bias, w_branch_a, w_branch_b, w_out, g_ffn, w_gate, w_up, w_down, g_final):
    assert g_mix.shape[0] == 1, "depth-1 block"
    D = x_prompt.shape[-1]
    Dp = pool_scale.shape[1]
    C = hyena_bias.shape[2]
    s1, s2, s3 = Dp, Dp + 3 * C, Dp + 3 * C + D
    cast = lambda t: t.astype(MXU_DTYPE)
    w = {
        "g_mix": g_mix[0], "w_pool": cast(w_in[0][:, :s1]), "w_hy": cast(w_in[0][:, s1:s2]),
        "w_ga": cast(w_in[0][:, s2:s3]), "w_gb": cast(w_in[0][:, s3:]),
        "pool_w": cast(pool_w[0]), "pool_scale": pool_scale[0],
        "conv_w": conv_w[0], "conv_b": conv_b[0],
        "filt": (filt_w1[0], filt_b1[0], filt_freq1[0], filt_w2[0], filt_b2[0], filt_freq2[0], filt_w3[0]),
        "hyena_bias": hyena_bias[0],
        "wa": cast(w_branch_a[0]), "wb": cast(w_branch_b[0]), "wo": cast(w_out[0]),
        "g_ffn": g_ffn[0], "wg": cast(w_gate[0]), "wu": cast(w_up[0]), "wd": cast(w_down[0]),
        "g_final": g_final,
    }
    return (_trunk(x_prompt, w), _trunk(x_sample, w))
```

```python
import functools
import math

import jax
import jax.numpy as jnp
from jax import lax
from jax.experimental import pallas as pl
from jax.experimental.pallas import tpu as pltpu

F32 = jnp.float32
MXU_DTYPE = jnp.bfloat16

EPS = 1e-6
POOL_WINDOWS = (2, 4, 8, 16)
DECAY_TARGET = 1e-2
FAST_DECAY_PCT = 0.3
SLOW_DECAY_PCT = 1.5

LANES = 128
NF = 128
HALO = 16
VMEM_LIMIT = 56 * 1024 * 1024


def _pick(n, pref, mult=8):
    best = None
    for d in range(mult, min(n, pref) + 1, mult):
        if n % d == 0:
            best = d
    return best if best is not None else n


def _params(*sem):
    return pltpu.CompilerParams(dimension_semantics=sem, vmem_limit_bytes=VMEM_LIMIT)


def _dot(a, b, **kw):
    return jnp.dot(a, b, preferred_element_type=F32, **kw)


def _rms(x):
    return x * lax.rsqrt(jnp.mean(x * x, axis=-1, keepdims=True) + EPS)


def _norm_kernel(x_ref, g_ref, o_ref):
    o_ref[...] = (_rms(x_ref[...]) * g_ref[...]).astype(o_ref.dtype)


def _norm_call(x2, g):
    T, D = x2.shape
    tm = _pick(T, 512)
    return pl.pallas_call(
        _norm_kernel,
        grid=(T // tm,),
        in_specs=[pl.BlockSpec((tm, D), lambda i: (i, 0)),
                  pl.BlockSpec((1, D), lambda i: (0, 0))],
        out_specs=pl.BlockSpec((tm, D), lambda i: (i, 0)),
        out_shape=jax.ShapeDtypeStruct((T, D), MXU_DTYPE),
        compiler_params=_params("parallel"),
        name="norm",
    )(x2, g.reshape(1, D))


def _halo_specs(tm, T, D, nidx):
    r = tm // HALO
    last = T // HALO - 1

    def pad(f):
        return (lambda i: f(i)) if nidx == 1 else (lambda i, j: f(i))

    return [
        pl.BlockSpec((HALO, D), pad(lambda i: (jnp.maximum(i * r - 1, 0), 0))),
        pl.BlockSpec((tm, D), pad(lambda i: (i, 0))),
        pl.BlockSpec((HALO, D), pad(lambda i: (jnp.minimum((i + 1) * r, last), 0))),
    ]


def _fill_hcat(hcat_ref, hp_ref, hm_ref, hn_ref, tm):
    hcat_ref[pl.ds(0, HALO), :] = hp_ref[...]
    hcat_ref[pl.ds(HALO, tm), :] = hm_ref[...]
    hcat_ref[pl.ds(HALO + tm, HALO), :] = hn_ref[...]


def _pool_chunks(tm):
    rows = tm + 2 * HALO
    ch = min(tm, 256)
    kw = min(rows, 512)
    return ch, kw, [(c * ch, min(c * ch, rows - kw)) for c in range(tm // ch)]


def _pool_kernel(hp_ref, hm_ref, hn_ref, w_ref, pw_ref, ps_ref, o_ref, hcat_ref, band_ref, *, tm, L):
    rows = tm + 2 * HALO
    gw = pw_ref.shape[-1]
    ch, kw, chunks = _pool_chunks(tm)

    @pl.when(pl.program_id(0) == 0)
    def _():
        r = lax.broadcasted_iota(jnp.int32, (ch, kw), 0)
        s = lax.broadcasted_iota(jnp.int32, (ch, kw), 1)
        for g, w in enumerate(POOL_WINDOWS):
            for c, (r0, s0) in enumerate(chunks):
                d = (s + (s0 - HALO)) - (r + r0)
                band_ref[g, c] = jnp.where((d >= -(w // 2)) & (d < w - w // 2), 1.0, 0.0).astype(MXU_DTYPE)

    _fill_hcat(hcat_ref, hp_ref, hm_ref, hn_ref, tm)
    p = _dot(hcat_ref[...], w_ref[...])
    t0 = (pl.program_id(0) * tm) % L
    tp = t0 - HALO + lax.broadcasted_iota(jnp.int32, (rows, 1), 0)
    p = jnp.where((tp >= 0) & (tp < L), p, 0.0)
    p_hi = p.astype(MXU_DTYPE)
    p_lo = (p - p_hi.astype(F32)).astype(MXU_DTYPE)
    tc = t0 + lax.broadcasted_iota(jnp.int32, (tm, 1), 0)
    for g, w in enumerate(POOL_WINDOWS):
        lo_off, hi_off = w // 2, w - w // 2
        sl = slice(g * gw, (g + 1) * gw)
        wsum = jnp.concatenate(
            [_dot(band_ref[g, c], p_hi[s0:s0 + kw, sl]) + _dot(band_ref[g, c], p_lo[s0:s0 + kw, sl])
             for c, (_, s0) in enumerate(chunks)], axis=0)
        cnt = (jnp.minimum(tc + hi_off, L) - jnp.maximum(tc - lo_off, 0)).astype(F32)
        pooled = wsum / cnt - p[HALO:HALO + tm, sl]
        mixed = _dot(pooled.astype(MXU_DTYPE), pw_ref[g]) * ps_ref[:, sl]
        o_ref[:, sl] = mixed.astype(o_ref.dtype)


def _pool_call(h, w_pool, pool_w, pool_scale, L):
    T, D = h.shape
    Dp = w_pool.shape[1]
    G, gw, _ = pool_w.shape
    tm = _pick(L, 512, HALO)
    ch, kw, chunks = _pool_chunks(tm)
    return pl.pallas_call(
        functools.partial(_pool_kernel, tm=tm, L=L),
        grid=(T // tm,),
        in_specs=_halo_specs(tm, T, D, 1) + [
            pl.BlockSpec((D, Dp), lambda i: (0, 0)),
            pl.BlockSpec((G, gw, gw), lambda i: (0, 0, 0)),
            pl.BlockSpec((1, Dp), lambda i: (0, 0)),
        ],
        out_specs=pl.BlockSpec((tm, Dp), lambda i: (i, 0)),
        out_shape=jax.ShapeDtypeStruct((T, Dp), MXU_DTYPE),
        scratch_shapes=[pltpu.VMEM((tm + 2 * HALO, D), MXU_DTYPE),
                        pltpu.VMEM((len(POOL_WINDOWS), len(chunks), ch, kw), MXU_DTYPE)],
        compiler_params=_params("arbitrary"),
        name="pool",
    )(h, h, h, w_pool, pool_w, pool_scale.reshape(1, Dp))


def _hyproj_kernel(hp_ref, hm_ref, hn_ref, w_ref, cw_ref, cb_ref, o_ref, hcat_ref, *, tm, L):
    rows = tm + 2 * HALO

    @pl.when(pl.program_id(1) == 0)
    def _():
        _fill_hcat(hcat_ref, hp_ref, hm_ref, hn_ref, tm)

    p = _dot(hcat_ref[...], w_ref[...])
    t0 = (pl.program_id(0) * tm) % L
    tp = t0 - HALO + lax.broadcasted_iota(jnp.int32, (rows, 1), 0)
    p = jnp.where((tp >= 0) & (tp < L), p, 0.0)
    prev = pltpu.roll(p, 1, 0)
    nxt = pltpu.roll(p, rows - 1, 0)
    y = prev * cw_ref[0:1, :] + p * cw_ref[1:2, :] + nxt * cw_ref[2:3, :] + cb_ref[...]
    o_ref[...] = y[HALO:HALO + tm]


def _hyproj_call(h, w_hy, conv_w, conv_b, L, C):
    T, D = h.shape
    n3 = w_hy.shape[1]
    tn = C if C % LANES == 0 else n3
    tm = _pick(L, 512, HALO)
    return pl.pallas_call(
        functools.partial(_hyproj_kernel, tm=tm, L=L),
        grid=(T // tm, n3 // tn),
        in_specs=_halo_specs(tm, T, D, 2) + [
            pl.BlockSpec((D, tn), lambda i, j: (0, j)),
            pl.BlockSpec((conv_w.shape[0], tn), lambda i, j: (0, j)),
            pl.BlockSpec((1, tn), lambda i, j: (0, j)),
        ],
        out_specs=pl.BlockSpec((tm, tn), lambda i, j: (i, j)),
        out_shape=jax.ShapeDtypeStruct((T, n3), F32),
        scratch_shapes=[pltpu.VMEM((tm + 2 * HALO, D), MXU_DTYPE)],
        compiler_params=_params("parallel", "arbitrary"),
        name="hyproj",
    )(h, h, h, w_hy, conv_w, conv_b.reshape(1, n3))


def _filter_kernel(w1_ref, b1_ref, f1_ref, w2_ref, b2_ref, f2_ref, w3_ref, two_ref, ss_ref,
                   *, L, C, fb, tmf):
    i = pl.program_id(0)
    N = 2 * L
    hp = lax.Precision.HIGHEST
    m = i * tmf + lax.broadcasted_iota(jnp.int32, (tmf, 1), 0)
    tf = jnp.where(m < L, m, N - m).astype(F32)
    tt = tf / (L - 1)
    wt = (2.0 * math.pi) * tf / L
    lane = lax.broadcasted_iota(jnp.int32, (1, LANES), 1)
    bidx = jnp.where(lane <= fb, lane - 1, lane - 1 - fb).astype(F32)
    bands = 1e-4 + bidx * ((fb - 1 - 1e-4) / (fb - 1))
    ang = bands * wt
    z = jnp.where(lane == 0, tt,
                  jnp.where(lane <= fb, jnp.cos(ang),
                            jnp.where(lane <= 2 * fb, -jnp.sin(ang), 0.0)))
    h = jnp.sin(f1_ref[...] * (_dot(z, w1_ref[...], precision=hp) + b1_ref[...]))
    h = jnp.sin(f2_ref[...] * (_dot(h, w2_ref[...], precision=hp) + b2_ref[...]))
    h_hi = h.astype(MXU_DTYPE)
    h_lo = (h - h_hi.astype(F32)).astype(MXU_DTYPE)
    hk = jnp.concatenate([h_hi, h_hi, h_lo], axis=1)
    max_decay = math.log(DECAY_TARGET) / FAST_DECAY_PCT
    min_decay = math.log(DECAY_TARGET) / SLOW_DECAY_PCT
    cl = lax.broadcasted_iota(jnp.int32, (1, C), 1).astype(F32)
    deltas = jnp.abs(min_decay + cl * ((max_decay - min_decay) / (C - 1)))
    decay = jnp.where(m == L, 0.0, jnp.exp(-tt * deltas))
    decay2 = jnp.concatenate([decay, decay], axis=1)

    def emit(two):
        two_ref[0] = two[:, :C]
        two_ref[1] = two[:, C:]
        ss_ref[...] += jnp.sum(two * two, axis=0, keepdims=True)

    @pl.when(i == 0)
    def _():
        ss_ref[...] = jnp.zeros_like(ss_ref)
        lag0 = jnp.where(m == 0, 1.0, 0.0)
        emit(decay2 * (_dot(hk, w3_ref[0]) + lag0 * _dot(hk, w3_ref[1])))

    @pl.when(i != 0)
    def _():
        emit(decay2 * _dot(hk, w3_ref[(i * tmf >= L).astype(jnp.int32)]))


def _filter_call(L, C, w1, b1, f1, w2, b2, f2, w3):
    N = 2 * L
    fe, fh = w1.shape
    fb = (fe - 1) // 2
    assert fe <= LANES
    w1p = jnp.zeros((LANES, fh), F32).at[:fe].set(w1)
    tmf = _pick(L, 512)
    w3d = jnp.transpose(w3.reshape(fh, 2, 2, C), (2, 0, 1, 3)).reshape(2, fh, 2 * C)
    w3_hi = w3d.astype(MXU_DTYPE)
    w3_lo = (w3d - w3_hi.astype(F32)).astype(MXU_DTYPE)
    w3k = jnp.concatenate([w3_hi, w3_lo, w3_hi], axis=1)
    full = lambda a: pl.BlockSpec(a.shape, lambda i: (0,) * a.ndim)
    args = (w1p, b1.reshape(1, fh), f1.reshape(1, fh), w2, b2.reshape(1, fh), f2.reshape(1, fh), w3k)
    two, ss = pl.pallas_call(
        functools.partial(_filter_kernel, L=L, C=C, fb=fb, tmf=tmf),
        grid=(N // tmf,),
        in_specs=[full(a) for a in args],
        out_specs=[pl.BlockSpec((2, tmf, C), lambda i: (0, i, 0)),
                   pl.BlockSpec((1, 2 * C), lambda i: (0, 0))],
        out_shape=[jax.ShapeDtypeStruct((2, N, C), F32),
                   jax.ShapeDtypeStruct((1, 2 * C), F32)],
        compiler_params=_params("arbitrary"),
        name="filter",
    )(*args)
    return two, ss.reshape(2, 1, C)


def _stack(mr, mi):
    return jnp.concatenate([jnp.concatenate([mr, -mi], axis=-1),
                            jnp.concatenate([mi, mr], axis=-1)], axis=-2)


def _il_rows(m):
    n = m.shape[-2] // 2
    return jnp.stack([m[..., :n, :], m[..., n:, :]], axis=-2).reshape(m.shape)


def _il_cols(m):
    return jnp.swapaxes(_il_rows(jnp.swapaxes(m, -1, -2)), -1, -2)


def _fft_tables(nc):
    N = nc * NF
    na = nc // 2
    i32 = jnp.int32
    k1 = jnp.arange(nc, dtype=i32)
    b = jnp.arange(NF, dtype=i32)
    e = (k1[None, :, None] * k1[None, None, :] * NF + b[:, None, None] * k1[None, :, None]) % N
    ang = (-2.0 * math.pi / N) * e.astype(F32)
    ar, ai = jnp.cos(ang), jnp.sin(ang)
    a_data = _il_rows(_stack(ar[:, :, :na], ai[:, :, :na]))
    a_filt = _il_rows(jnp.concatenate([ar, ai], axis=1))
    cr, ci = jnp.swapaxes(ar, 1, 2)[:, :na] / nc, -jnp.swapaxes(ai, 1, 2)[:, :na] / nc
    c_inv = _il_cols(_stack(cr, ci))
    ang = (-2.0 * math.pi / NF) * ((b[:, None] * b[None, :]) % NF).astype(F32)
    gr, gi = jnp.cos(ang), jnp.sin(ang)
    g_fwd = _il_cols(_stack(gr, gi))
    g_inv = _il_rows(_stack(gr.T / NF, -gi.T / NF))
    cast = lambda t: t.astype(MXU_DTYPE)
    return cast(a_data), cast(a_filt), cast(g_fwd), cast(g_inv), cast(c_inv)


NFP = NF + 8
U32 = jnp.uint32
PACKED = True


def _fine_per_step(nc):
    return max(8, min(32, (16 * 128) // nc))


def _rows2d(ref):
    return ref.reshape(math.prod(ref.shape[:-1]), ref.shape[-1])


def _ld_rows(ref, start, n, stride):
    return _rows2d(ref)[pl.ds(start, n, stride=stride), :]


def _st_rows(ref, start, n, stride, val):
    _rows2d(ref)[pl.ds(start, n, stride=stride), :] = val


def _spec_scratch(nc):
    if PACKED:
        return pltpu.VMEM((2, 1, nc * NFP, LANES), U32)
    return pltpu.VMEM((2, 2, nc * NFP, LANES), F32)


def _spec_store(y_ref, rows, v):
    if PACKED:
        w = pltpu.bitcast(v.astype(jnp.bfloat16), U32)
        y_ref[0, 0, rows, :] = w[:, :LANES]
        y_ref[1, 0, rows, :] = w[:, LANES:]
    else:
        for half in range(2):
            lanes = slice(half * LANES, (half + 1) * LANES)
            y_ref[half, 0, rows, :] = v[0::2, lanes]
            y_ref[half, 1, rows, :] = v[1::2, lanes]


def _spec_load(y_ref, rows):
    if PACKED:
        w = jnp.concatenate([y_ref[0, 0, rows, :], y_ref[1, 0, rows, :]], axis=1)
        return pltpu.bitcast(w, jnp.bfloat16)
    re = jnp.concatenate([y_ref[0, 0, rows, :], y_ref[1, 0, rows, :]], axis=1)
    im = jnp.concatenate([y_ref[0, 1, rows, :], y_ref[1, 1, rows, :]], axis=1)
    return jnp.stack([re, im], axis=1).reshape(2 * re.shape[0], re.shape[1]).astype(MXU_DTYPE)


def _phase_a(z_refs, at_ref, y_ref, step, *, planes, na, nc):
    bb = at_ref.shape[0]
    for b in range(bb):
        x = jnp.concatenate(
            [jnp.concatenate([_ld_rows(z, p * na * bb + b, na, bb) for p in range(planes)], axis=0)
             for z in z_refs], axis=1)
        r = _dot(at_ref[b], x.astype(MXU_DTYPE))
        _spec_store(y_ref, pl.ds(step * bb + b, nc, stride=NFP), r)


def _k1_rows(step, kk, k):
    return pl.ds(pl.multiple_of((step * kk + k) * NFP, 8), NF)


def _lconv_kernel(z0_ref, z1_ref, gate_ref, at_ref, ct_ref, gf_ref, gi_ref, hr_ref, hi_ref,
                  o_ref, y_ref, yc_ref, *, na, nc, kk, sa, sb):
    s = pl.program_id(2)

    @pl.when(s < sa)
    def _():
        _phase_a((z0_ref, z1_ref), at_ref, y_ref, s, planes=2, na=na, nc=nc)

    @pl.when((s >= sa) & (s < sa + sb))
    def _():
        for k in range(kk):
            rows = _k1_rows(s - sa, kk, k)
            z = _dot(gf_ref[...], _spec_load(y_ref, rows))
            zr, zi = z[:NF], z[NF:]
            hr, hi = hr_ref[k], hi_ref[k]
            x2 = jnp.concatenate([zr * hr - zi * hi, zr * hi + zi * hr], axis=0).astype(MXU_DTYPE)
            u = _dot(gi_ref[...], x2)
            _spec_store(y_ref, rows, u)

    @pl.when(s >= sa + sb)
    def _():
        step = s - sa - sb
        bb, ycp = ct_ref.shape[0], yc_ref.shape[2]
        for b in range(bb):
            x = _spec_load(y_ref, pl.ds(step * bb + b, nc, stride=NFP))
            r = _dot(ct_ref[b], x)
            for half in range(2):
                _st_rows(yc_ref.at[half], b, 2 * na, ycp, r[:, half * LANES:(half + 1) * LANES])
        for half in range(2):
            lanes = slice(half * LANES, (half + 1) * LANES)
            y = yc_ref[half, :, :bb, :].reshape(2, na, bb, LANES)
            o_ref[:, :, :, lanes] = (gate_ref[:, :, :, lanes] * y).astype(o_ref.dtype)


def _lconv_call(z5, z_off, gate5, gate_off, spec, order, tabs, na, out_dtype):
    a_data, _, g_fwd, g_inv, c_inv = tabs
    P = z5.shape[0]
    C = spec[0].shape[-1]
    nc = 2 * na
    ct = 2 * LANES
    kk = _pick(nc, 16, 1)
    BB = _fine_per_step(nc)
    YCP = BB + 8
    sa, sb = NF // BB, nc // kk
    zo, go = z_off // LANES, gate_off // ct
    ia = lambda s: jnp.minimum(s, sa - 1)
    ib = lambda s: jnp.clip(s - sa, 0, sb - 1)
    ic = lambda s: jnp.maximum(s - sa - sb, 0)
    zspec = lambda h: pl.BlockSpec((None, 2, na, BB, LANES), lambda p, jc, s: (p, 0, 0, ia(s), zo + 2 * jc + h))
    hspec = pl.BlockSpec((None, kk, NF, ct), lambda p, jc, s: (order, ib(s), 0, jc))
    gspec = pl.BlockSpec((2 * NF, 2 * NF), lambda p, jc, s: (0, 0))
    return pl.pallas_call(
        functools.partial(_lconv_kernel, na=na, nc=nc, kk=kk, sa=sa, sb=sb),
        grid=(P, C // ct, 2 * sa + sb),
        in_specs=[zspec(0), zspec(1),
                  pl.BlockSpec((None, 2, na, BB, ct), lambda p, jc, s: (p, 0, 0, ic(s), go + jc)),
                  pl.BlockSpec((BB, 2 * nc, nc), lambda p, jc, s: (ia(s), 0, 0)),
                  pl.BlockSpec((BB, nc, 2 * nc), lambda p, jc, s: (ic(s), 0, 0)),
                  gspec, gspec, hspec, hspec],
        out_specs=pl.BlockSpec((None, 2, na, BB, ct), lambda p, jc, s: (p, 0, 0, ic(s), jc)),
        out_shape=jax.ShapeDtypeStruct((P, 2, na, NF, C), out_dtype),
        scratch_shapes=[_spec_scratch(nc), pltpu.VMEM((2, 2 * na, YCP, LANES), F32)],
        compiler_params=_params("parallel", "parallel", "arbitrary"),
        name="lconv",
    )(z5, z5, gate5, a_data, c_inv, g_fwd, g_inv, spec[0], spec[1])


def _fspec_kernel(t0_ref, t1_ref, at_ref, gf_ref, ss_ref, bias_ref, hr_ref, hi_ref, y_ref,
                  *, nc, kk, sa):
    s = pl.program_id(2)

    @pl.when(s < sa)
    def _():
        _phase_a((t0_ref, t1_ref), at_ref, y_ref, s, planes=1, na=nc, nc=nc)

    @pl.when(s >= sa)
    def _():
        scale = lax.rsqrt(ss_ref[...] + EPS)
        for k in range(kk):
            z = _dot(gf_ref[...], _spec_load(y_ref, _k1_rows(s - sa, kk, k)))
            hr_ref[k] = z[:NF] * scale + bias_ref[...]
            hi_ref[k] = z[NF:] * scale


def _fspec_call(two, ss, bias, tabs, nc):
    _, a_filt, g_fwd, _, _ = tabs
    C = two.shape[-1]
    ct = 2 * LANES
    kk = _pick(nc, 16, 1)
    BB = _fine_per_step(nc)
    sa, sb = NF // BB, nc // kk
    two5 = two.reshape(2, 1, nc, NF, C)
    ia = lambda s: jnp.minimum(s, sa - 1)
    ib = lambda s: jnp.maximum(s - sa, 0)
    tspec = lambda h: pl.BlockSpec((None, 1, nc, BB, LANES), lambda o, jc, s: (o, 0, 0, ia(s), 2 * jc + h))
    vspec = pl.BlockSpec((None, 1, ct), lambda o, jc, s: (o, 0, jc))
    hspec = pl.BlockSpec((None, kk, NF, ct), lambda o, jc, s: (o, ib(s), 0, jc))
    out = jax.ShapeDtypeStruct((2, nc, NF, C), F32)
    return pl.pallas_call(
        functools.partial(_fspec_kernel, nc=nc, kk=kk, sa=sa),
        grid=(2, C // ct, sa + sb),
        in_specs=[tspec(0), tspec(1),
                  pl.BlockSpec((BB, 2 * nc, nc), lambda o, jc, s: (ia(s), 0, 0)),
                  pl.BlockSpec((2 * NF, 2 * NF), lambda o, jc, s: (0, 0)),
                  vspec, vspec],
        out_specs=[hspec, hspec],
        out_shape=[out, out],
        scratch_shapes=[_spec_scratch(nc)],
        compiler_params=_params("parallel", "parallel", "arbitrary"),
        name="fspec",
    )(two5, two5, a_filt, g_fwd, ss, bias)


def _merge_kernel(h_ref, a_ref, b_ref, x_ref, wga_ref, wgb_ref, wa_ref, wb_ref, wo_ref, o_ref):
    h = h_ref[...]
    ga = jax.nn.sigmoid(_dot(h, wga_ref[...]))
    gb = jax.nn.sigmoid(_dot(h, wgb_ref[...]))
    m = ga * _dot(a_ref[...], wa_ref[...]) + gb * _dot(b_ref[...], wb_ref[...])
    contrib = _dot(m.astype(MXU_DTYPE), wo_ref[...])

    @pl.when(pl.program_id(1) == 0)
    def _():
        o_ref[...] = x_ref[...] + contrib

    @pl.when(pl.program_id(1) != 0)
    def _():
        o_ref[...] += contrib


def _merge_call(h, a, b, x2, w_ga, w_gb, wa, wb, wo):
    T, D = x2.shape
    Dp, C = a.shape[1], b.shape[1]
    tm = _pick(T, 512)
    tn = _pick(D, 512, LANES)
    return pl.pallas_call(
        _merge_kernel,
        grid=(T // tm, D // tn),
        in_specs=[pl.BlockSpec((tm, D), lambda i, j: (i, 0)),
                  pl.BlockSpec((tm, Dp), lambda i, j: (i, 0)),
                  pl.BlockSpec((tm, C), lambda i, j: (i, 0)),
                  pl.BlockSpec((tm, D), lambda i, j: (i, 0)),
                  pl.BlockSpec((D, tn), lambda i, j: (0, j)),
                  pl.BlockSpec((D, tn), lambda i, j: (0, j)),
                  pl.BlockSpec((Dp, tn), lambda i, j: (0, j)),
                  pl.BlockSpec((C, tn), lambda i, j: (0, j)),
                  pl.BlockSpec((tn, D), lambda i, j: (j, 0))],
        out_specs=pl.BlockSpec((tm, D), lambda i, j: (i, 0)),
        out_shape=jax.ShapeDtypeStruct((T, D), F32),
        compiler_params=_params("parallel", "arbitrary"),
        name="merge",
    )(h, a, b, x2, w_ga, w_gb, wa, wb, wo)


def _ffn_kernel(x_ref, g_ref, wg_ref, wu_ref, wd_ref, gf_ref, o_ref, h_ref, acc_ref):
    j = pl.program_id(1)

    @pl.when(j == 0)
    def _():
        x = x_ref[...]
        h_ref[...] = (_rms(x) * g_ref[...]).astype(h_ref.dtype)
        acc_ref[...] = x

    h = h_ref[...]
    act = jax.nn.silu(_dot(h, wg_ref[...])) * _dot(h, wu_ref[...])
    acc_ref[...] += _dot(act.astype(MXU_DTYPE), wd_ref[...])

    @pl.when(j == pl.num_programs(1) - 1)
    def _():
        o_ref[...] = _rms(acc_ref[...]) * gf_ref[...]


def _ffn_call(x2, g, wg, wu, wd, gf):
    T, D = x2.shape
    Dff = wg.shape[1]
    tm = _pick(T, 512)
    tf = _pick(Dff, 512, LANES)
    return pl.pallas_call(
        _ffn_kernel,
        grid=(T // tm, Dff // tf),
        in_specs=[pl.BlockSpec((tm, D), lambda i, j: (i, 0)),
                  pl.BlockSpec((1, D), lambda i, j: (0, 0)),
                  pl.BlockSpec((D, tf), lambda i, j: (0, j)),
                  pl.BlockSpec((D, tf), lambda i, j: (0, j)),
                  pl.BlockSpec((tf, D), lambda i, j: (j, 0)),
                  pl.BlockSpec((1, D), lambda i, j: (0, 0))],
        out_specs=pl.BlockSpec((tm, D), lambda i, j: (i, 0)),
        out_shape=jax.ShapeDtypeStruct((T, D), F32),
        scratch_shapes=[pltpu.VMEM((tm, D), MXU_DTYPE), pltpu.VMEM((tm, D), F32)],
        compiler_params=_params("parallel", "arbitrary"),
        name="ffn",
    )(x2, g.reshape(1, D), wg, wu, wd, gf.reshape(1, D))


def _trunk(x, w):
    B, L, D = x.shape
    T = B * L
    C = w["hyena_bias"].shape[1]
    assert B % 2 == 0 and L % NF == 0 and C % (2 * LANES) == 0 and w["hyena_bias"].shape[0] == 2
    na = L // NF
    nc = 2 * na
    P = B // 2
    x2 = x.reshape(T, D)

    h = _norm_call(x2, w["g_mix"])
    a = _pool_call(h, w["w_pool"], w["pool_w"], w["pool_scale"], L)
    uc = _hyproj_call(h, w["w_hy"], w["conv_w"], w["conv_b"], L, C)
    uc5 = uc.reshape(P, 2, na, NF, 3 * C)

    tabs = _fft_tables(nc)
    two, ss = _filter_call(L, C, *w["filt"])
    spec = _fspec_call(two, ss, w["hyena_bias"].reshape(2, 1, C), tabs, nc)

    z1 = _lconv_call(uc5, 0, uc5, C, spec, 0, tabs, na, F32)
    b = _lconv_call(z1, 0, uc5, 2 * C, spec, 1, tabs, na, MXU_DTYPE).reshape(T, C)

    xn = _merge_call(h, a, b, x2, w["w_ga"], w["w_gb"], w["wa"], w["wb"], w["wo"])
    y = _ffn_call(xn, w["g_ffn"], w["wg"], w["wu"], w["wd"], w["g_final"])
    return y.reshape(B, L, D)


def kernel(x_prompt, x_sample, g_mix, w_in, pool_w, pool_scale, conv_w, conv_b, filt_w1, filt_b1, filt_freq1, filt_w2, filt_b2, filt_freq2, filt_w3, hyena_bias, w_branch_a, w_branch_b, w_out, g_ffn, w_gate, w_up, w_down, g_final):
    assert g_mix.shape[0] == 1, "depth-1 block"
    D = x_prompt.shape[-1]
    Dp = pool_scale.shape[1]
    C = hyena_bias.shape[2]
    s1, s2, s3 = Dp, Dp + 3 * C, Dp + 3 * C + D
    cast = lambda t: t.astype(MXU_DTYPE)
    w = {
        "g_mix": g_mix[0], "w_pool": cast(w_in[0][:, :s1]), "w_hy": cast(w_in[0][:, s1:s2]),
        "w_ga": cast(w_in[0][:, s2:s3]), "w_gb": cast(w_in[0][:, s3:]),
        "pool_w": cast(pool_w[0]), "pool_scale": pool_scale[0],
        "conv_w": conv_w[0], "conv_b": conv_b[0],
        "filt": (filt_w1[0], filt_b1[0], filt_freq1[0], filt_w2[0], filt_b2[0], filt_freq2[0], filt_w3[0]),
        "hyena_bias": hyena_bias[0],
        "wa": cast(w_branch_a[0]), "wb": cast(w_branch_b[0]), "wo": cast(w_out[0]),
        "g_ffn": g_ffn[0], "wg": cast(w_gate[0]), "wu": cast(w_up[0]), "wd": cast(w_down[0]),
        "g_final": g_final,
    }
    return (_trunk(x_prompt, w), _trunk(x_sample, w))
```

```python
import functools
import math

import jax
import jax.numpy as jnp
from jax import lax
from jax.experimental import pallas as pl
from jax.experimental.pallas import tpu as pltpu

F32 = jnp.float32
MXU_DTYPE = jnp.bfloat16

EPS = 1e-6
POOL_WINDOWS = (2, 4, 8, 16)
DECAY_TARGET = 1e-2
FAST_DECAY_PCT = 0.3
SLOW_DECAY_PCT = 1.5

LANES = 128
NF = 128
HALO = 16
VMEM_LIMIT = 56 * 1024 * 1024


def _pick(n, pref, mult=8):
    best = None
    for d in range(mult, min(n, pref) + 1, mult):
        if n % d == 0:
            best = d
    return best if best is not None else n


def _params(*sem):
    return pltpu.CompilerParams(dimension_semantics=sem, vmem_limit_bytes=VMEM_LIMIT)


def _dot(a, b, **kw):
    return jnp.dot(a, b, preferred_element_type=F32, **kw)


def _rms(x):
    return x * lax.rsqrt(jnp.mean(x * x, axis=-1, keepdims=True) + EPS)


def _norm_kernel(x_ref, g_ref, o_ref):
    o_ref[...] = (_rms(x_ref[...]) * g_ref[...]).astype(o_ref.dtype)


def _norm_call(x2, g):
    T, D = x2.shape
    tm = _pick(T, 512)
    return pl.pallas_call(
        _norm_kernel,
        grid=(T // tm,),
        in_specs=[pl.BlockSpec((tm, D), lambda i: (i, 0)),
                  pl.BlockSpec((1, D), lambda i: (0, 0))],
        out_specs=pl.BlockSpec((tm, D), lambda i: (i, 0)),
        out_shape=jax.ShapeDtypeStruct((T, D), MXU_DTYPE),
        compiler_params=_params("parallel"),
        name="norm",
    )(x2, g.reshape(1, D))


def _halo_specs(tm, T, D, tile_of):
    r = tm // HALO
    last = T // HALO - 1
    return [
        pl.BlockSpec((HALO, D), lambda *g: (jnp.maximum(tile_of(*g) * r - 1, 0), 0)),
        pl.BlockSpec((tm, D), lambda *g: (tile_of(*g), 0)),
        pl.BlockSpec((HALO, D), lambda *g: (jnp.minimum((tile_of(*g) + 1) * r, last), 0)),
    ]


def _fill_hcat(hcat_ref, hp_ref, hm_ref, hn_ref, tm):
    hcat_ref[pl.ds(0, HALO), :] = hp_ref[...]
    hcat_ref[pl.ds(HALO, tm), :] = hm_ref[...]
    hcat_ref[pl.ds(HALO + tm, HALO), :] = hn_ref[...]


def _pool_chunks(tm):
    rows = tm + 2 * HALO
    ch = min(tm, 256)
    kw = min(rows, 512)
    return ch, kw, [(c * ch, min(c * ch, rows - kw)) for c in range(tm // ch)]


def _pool_kernel(hp_ref, hm_ref, hn_ref, w_ref, pw_ref, ps_ref, o_ref, hcat_ref, band_ref, *, tm, L):
    rows = tm + 2 * HALO
    gw = pw_ref.shape[-1]
    ch, kw, chunks = _pool_chunks(tm)

    @pl.when(pl.program_id(0) == 0)
    def _():
        r = lax.broadcasted_iota(jnp.int32, (ch, kw), 0)
        s = lax.broadcasted_iota(jnp.int32, (ch, kw), 1)
        for g, w in enumerate(POOL_WINDOWS):
            for c, (r0, s0) in enumerate(chunks):
                d = (s + (s0 - HALO)) - (r + r0)
                band_ref[g, c] = jnp.where((d >= -(w // 2)) & (d < w - w // 2), 1.0, 0.0).astype(MXU_DTYPE)

    _fill_hcat(hcat_ref, hp_ref, hm_ref, hn_ref, tm)
    p = _dot(hcat_ref[...], w_ref[...])
    t0 = (pl.program_id(0) * tm) % L
    tp = t0 - HALO + lax.broadcasted_iota(jnp.int32, (rows, 1), 0)
    p = jnp.where((tp >= 0) & (tp < L), p, 0.0)
    p_hi = p.astype(MXU_DTYPE)
    p_lo = (p - p_hi.astype(F32)).astype(MXU_DTYPE)
    tc = t0 + lax.broadcasted_iota(jnp.int32, (tm, 1), 0)
    for g, w in enumerate(POOL_WINDOWS):
        lo_off, hi_off = w // 2, w - w // 2
        sl = slice(g * gw, (g + 1) * gw)
        wsum = jnp.concatenate(
            [_dot(band_ref[g, c], p_hi[s0:s0 + kw, sl]) + _dot(band_ref[g, c], p_lo[s0:s0 + kw, sl])
             for c, (_, s0) in enumerate(chunks)], axis=0)
        cnt = (jnp.minimum(tc + hi_off, L) - jnp.maximum(tc - lo_off, 0)).astype(F32)
        pooled = wsum / cnt - p[HALO:HALO + tm, sl]
        mixed = _dot(pooled.astype(MXU_DTYPE), pw_ref[g]) * ps_ref[:, sl]
        o_ref[:, sl] = mixed.astype(o_ref.dtype)


def _pool_call(h, w_pool, pool_w, pool_scale, L):
    T, D = h.shape
    Dp = w_pool.shape[1]
    G, gw, _ = pool_w.shape
    tm = _pick(L, 512, HALO)
    ch, kw, chunks = _pool_chunks(tm)
    return pl.pallas_call(
        functools.partial(_pool_kernel, tm=tm, L=L),
        grid=(T // tm,),
        in_specs=_halo_specs(tm, T, D, lambda i: i) + [
            pl.BlockSpec((D, Dp), lambda i: (0, 0)),
            pl.BlockSpec((G, gw, gw), lambda i: (0, 0, 0)),
            pl.BlockSpec((1, Dp), lambda i: (0, 0)),
        ],
        out_specs=pl.BlockSpec((tm, Dp), lambda i: (i, 0)),
        out_shape=jax.ShapeDtypeStruct((T, Dp), MXU_DTYPE),
        scratch_shapes=[pltpu.VMEM((tm + 2 * HALO, D), MXU_DTYPE),
                        pltpu.VMEM((len(POOL_WINDOWS), len(chunks), ch, kw), MXU_DTYPE)],
        compiler_params=_params("arbitrary"),
        name="pool",
    )(h, h, h, w_pool, pool_w, pool_scale.reshape(1, Dp))


def _hyproj_kernel(hp_ref, hm_ref, hn_ref, w_ref, cw_ref, cb_ref, o_ref, hcat_ref, *, tm, L):
    rows = tm + 2 * HALO

    @pl.when(pl.program_id(1) == 0)
    def _():
        _fill_hcat(hcat_ref, hp_ref, hm_ref, hn_ref, tm)

    t0 = (pl.program_id(0) * tm) % L
    tp = t0 - HALO + lax.broadcasted_iota(jnp.int32, (rows, 1), 0)
    p = jnp.where((tp >= 0) & (tp < L), _dot(hcat_ref[...], w_ref[...]), 0.0)
    prev = pltpu.roll(p, 1, 0)
    nxt = pltpu.roll(p, rows - 1, 0)
    y = prev * cw_ref[0:1, :] + p * cw_ref[1:2, :] + nxt * cw_ref[2:3, :] + cb_ref[...]
    o_ref[...] = y[HALO:HALO + tm]


def _hyproj_call(h, w_hy, conv_w, conv_b, L, C):
    T, D = h.shape
    n3 = w_hy.shape[1]
    tn = C if C % LANES == 0 else n3
    tm = _pick(L, 512, HALO)
    return pl.pallas_call(
        functools.partial(_hyproj_kernel, tm=tm, L=L),
        grid=(T // tm, n3 // tn),
        in_specs=_halo_specs(tm, T, D, lambda i, j: i) + [
            pl.BlockSpec((D, tn), lambda i, j: (0, j)),
            pl.BlockSpec((conv_w.shape[0], tn), lambda i, j: (0, j)),
            pl.BlockSpec((1, tn), lambda i, j: (0, j)),
        ],
        out_specs=pl.BlockSpec((tm, tn), lambda i, j: (i, j)),
        out_shape=jax.ShapeDtypeStruct((T, n3), F32),
        scratch_shapes=[pltpu.VMEM((tm + 2 * HALO, D), MXU_DTYPE)],
        compiler_params=_params("parallel", "arbitrary"),
        name="hyproj",
    )(h, h, h, w_hy, conv_w, conv_b.reshape(1, n3))


def _filter_kernel(w1_ref, b1_ref, f1_ref, w2_ref, b2_ref, f2_ref, w3_ref, two_ref, ss_ref,
                   *, L, C, fb, tmf):
    i = pl.program_id(0)
    N = 2 * L
    hp = lax.Precision.HIGHEST
    m = i * tmf + lax.broadcasted_iota(jnp.int32, (tmf, 1), 0)
    tf = jnp.where(m < L, m, N - m).astype(F32)
    tt = tf / (L - 1)
    wt = (2.0 * math.pi) * tf / L
    lane = lax.broadcasted_iota(jnp.int32, (1, LANES), 1)
    bidx = jnp.where(lane <= fb, lane - 1, lane - 1 - fb).astype(F32)
    bands = 1e-4 + bidx * ((fb - 1 - 1e-4) / (fb - 1))
    ang = bands * wt
    z = jnp.where(lane == 0, tt,
                  jnp.where(lane <= fb, jnp.cos(ang),
                            jnp.where(lane <= 2 * fb, -jnp.sin(ang), 0.0)))
    h = jnp.sin(f1_ref[...] * (_dot(z, w1_ref[...], precision=hp) + b1_ref[...]))
    h = jnp.sin(f2_ref[...] * (_dot(h, w2_ref[...], precision=hp) + b2_ref[...]))
    h_hi = h.astype(MXU_DTYPE)
    h_lo = (h - h_hi.astype(F32)).astype(MXU_DTYPE)
    hk = jnp.concatenate([h_hi, h_hi, h_lo], axis=1)
    max_decay = math.log(DECAY_TARGET) / FAST_DECAY_PCT
    min_decay = math.log(DECAY_TARGET) / SLOW_DECAY_PCT
    cl = lax.broadcasted_iota(jnp.int32, (1, C), 1).astype(F32)
    deltas = jnp.abs(min_decay + cl * ((max_decay - min_decay) / (C - 1)))
    decay = jnp.where(m == L, 0.0, jnp.exp(-tt * deltas))
    decay2 = jnp.concatenate([decay, decay], axis=1)

    def emit(two):
        two_ref[0] = two[:, :C]
        two_ref[1] = two[:, C:]
        ss_ref[...] += jnp.sum(two * two, axis=0, keepdims=True)

    @pl.when(i == 0)
    def _():
        ss_ref[...] = jnp.zeros_like(ss_ref)
        lag0 = jnp.where(m == 0, 1.0, 0.0)
        emit(decay2 * (_dot(hk, w3_ref[0]) + lag0 * _dot(hk, w3_ref[1])))

    @pl.when(i != 0)
    def _():
        emit(decay2 * _dot(hk, w3_ref[(i * tmf >= L).astype(jnp.int32)]))


def _filter_call(L, C, w1, b1, f1, w2, b2, f2, w3):
    N = 2 * L
    fe, fh = w1.shape
    fb = (fe - 1) // 2
    assert fe <= LANES
    w1p = jnp.zeros((LANES, fh), F32).at[:fe].set(w1)
    tmf = _pick(L, 512)
    w3d = jnp.transpose(w3.reshape(fh, 2, 2, C), (2, 0, 1, 3)).reshape(2, fh, 2 * C)
    w3_hi = w3d.astype(MXU_DTYPE)
    w3_lo = (w3d - w3_hi.astype(F32)).astype(MXU_DTYPE)
    w3k = jnp.concatenate([w3_hi, w3_lo, w3_hi], axis=1)
    full = lambda a: pl.BlockSpec(a.shape, lambda i: (0,) * a.ndim)
    args = (w1p, b1.reshape(1, fh), f1.reshape(1, fh), w2, b2.reshape(1, fh), f2.reshape(1, fh), w3k)
    two, ss = pl.pallas_call(
        functools.partial(_filter_kernel, L=L, C=C, fb=fb, tmf=tmf),
        grid=(N // tmf,),
        in_specs=[full(a) for a in args],
        out_specs=[pl.BlockSpec((2, tmf, C), lambda i: (0, i, 0)),
                   pl.BlockSpec((1, 2 * C), lambda i: (0, 0))],
        out_shape=[jax.ShapeDtypeStruct((2, N, C), F32),
                   jax.ShapeDtypeStruct((1, 2 * C), F32)],
        compiler_params=_params("arbitrary"),
        name="filter",
    )(*args)
    return two, ss.reshape(2, 1, C)


def _stack(mr, mi):
    return jnp.concatenate([jnp.concatenate([mr, -mi], axis=-1),
                            jnp.concatenate([mi, mr], axis=-1)], axis=-2)


def _il_rows(m):
    n = m.shape[-2] // 2
    return jnp.stack([m[..., :n, :], m[..., n:, :]], axis=-2).reshape(m.shape)


def _il_cols(m):
    return jnp.swapaxes(_il_rows(jnp.swapaxes(m, -1, -2)), -1, -2)


def _fft_tables(nc):
    N = nc * NF
    na = nc // 2
    i32 = jnp.int32
    k1 = jnp.arange(nc, dtype=i32)
    b = jnp.arange(NF, dtype=i32)
    e = (k1[None, :, None] * k1[None, None, :] * NF + b[:, None, None] * k1[None, :, None]) % N
    ang = (-2.0 * math.pi / N) * e.astype(F32)
    ar, ai = jnp.cos(ang), jnp.sin(ang)
    a_data = _il_rows(_stack(ar[:, :, :na], ai[:, :, :na]))
    a_filt = _il_rows(jnp.concatenate([ar, ai], axis=1))
    cr, ci = jnp.swapaxes(ar, 1, 2)[:, :na] / nc, -jnp.swapaxes(ai, 1, 2)[:, :na] / nc
    c_inv = _il_cols(_stack(cr, ci))
    ang = (-2.0 * math.pi / NF) * ((b[:, None] * b[None, :]) % NF).astype(F32)
    gr, gi = jnp.cos(ang), jnp.sin(ang)
    g_fwd = _il_cols(_stack(gr, gi))
    g_inv = _il_rows(_stack(gr.T / NF, -gi.T / NF))
    cast = lambda t: t.astype(MXU_DTYPE)
    return cast(a_data), cast(a_filt), cast(g_fwd), cast(g_inv), cast(c_inv)


NFP = NF + 8
U32 = jnp.uint32
PACKED = True


def _fine_per_step(nc):
    return max(8, min(32, (16 * 128) // nc))


def _rows2d(ref):
    return ref.reshape(math.prod(ref.shape[:-1]), ref.shape[-1])


def _ld_rows(ref, start, n, stride):
    return _rows2d(ref)[pl.ds(start, n, stride=stride), :]


def _st_rows(ref, start, n, stride, val):
    _rows2d(ref)[pl.ds(start, n, stride=stride), :] = val


def _spec_scratch(nc):
    if PACKED:
        return pltpu.VMEM((2, 1, nc * NFP, LANES), U32)
    return pltpu.VMEM((2, 2, nc * NFP, LANES), F32)


def _spec_store(y_ref, rows, v):
    if PACKED:
        w = pltpu.bitcast(v.astype(jnp.bfloat16), U32)
        y_ref[0, 0, rows, :] = w[:, :LANES]
        y_ref[1, 0, rows, :] = w[:, LANES:]
    else:
        for half in range(2):
            lanes = slice(half * LANES, (half + 1) * LANES)
            y_ref[half, 0, rows, :] = v[0::2, lanes]
            y_ref[half, 1, rows, :] = v[1::2, lanes]


def _spec_load(y_ref, rows):
    if PACKED:
        w = jnp.concatenate([y_ref[0, 0, rows, :], y_ref[1, 0, rows, :]], axis=1)
        return pltpu.bitcast(w, jnp.bfloat16)
    re = jnp.concatenate([y_ref[0, 0, rows, :], y_ref[1, 0, rows, :]], axis=1)
    im = jnp.concatenate([y_ref[0, 1, rows, :], y_ref[1, 1, rows, :]], axis=1)
    return jnp.stack([re, im], axis=1).reshape(2 * re.shape[0], re.shape[1]).astype(MXU_DTYPE)


def _phase_a(z_refs, at_ref, y_ref, step, *, planes, na, nc):
    bb = at_ref.shape[0]
    for b in range(bb):
        x = jnp.concatenate(
            [jnp.concatenate([_ld_rows(z, p * na * bb + b, na, bb) for p in range(planes)], axis=0)
             for z in z_refs], axis=1)
        r = _dot(at_ref[b], x.astype(MXU_DTYPE))
        _spec_store(y_ref, pl.ds(step * bb + b, nc, stride=NFP), r)


def _k1_rows(step, kk, k):
    return pl.ds(pl.multiple_of((step * kk + k) * NFP, 8), NF)


def _lconv_kernel(z0_ref, z1_ref, gate_ref, at_ref, ct_ref, gf_ref, gi_ref, hr_ref, hi_ref,
                  o_ref, y_ref, yc_ref, *, na, nc, kk, sa, sb):
    s = pl.program_id(2)

    @pl.when(s < sa)
    def _():
        _phase_a((z0_ref, z1_ref), at_ref, y_ref, s, planes=2, na=na, nc=nc)

    @pl.when((s >= sa) & (s < sa + sb))
    def _():
        for k in range(kk):
            rows = _k1_rows(s - sa, kk, k)
            z = _dot(gf_ref[...], _spec_load(y_ref, rows))
            zr, zi = z[:NF], z[NF:]
            hr, hi = hr_ref[k], hi_ref[k]
            x2 = jnp.concatenate([zr * hr - zi * hi, zr * hi + zi * hr], axis=0).astype(MXU_DTYPE)
            u = _dot(gi_ref[...], x2)
            _spec_store(y_ref, rows, u)

    @pl.when(s >= sa + sb)
    def _():
        step = s - sa - sb
        bb, ycp = ct_ref.shape[0], yc_ref.shape[2]
        for b in range(bb):
            x = _spec_load(y_ref, pl.ds(step * bb + b, nc, stride=NFP))
            r = _dot(ct_ref[b], x)
            for half in range(2):
                _st_rows(yc_ref.at[half], b, 2 * na, ycp, r[:, half * LANES:(half + 1) * LANES])
        for half in range(2):
            lanes = slice(half * LANES, (half + 1) * LANES)
            y = yc_ref[half, :, :bb, :].reshape(2, na, bb, LANES)
            o_ref[:, :, :, lanes] = (gate_ref[:, :, :, lanes] * y).astype(o_ref.dtype)


def _lconv_call(z5, z_off, gate5, gate_off, spec, order, tabs, na, out_dtype):
    a_data, _, g_fwd, g_inv, c_inv = tabs
    P = z5.shape[0]
    C = spec[0].shape[-1]
    nc = 2 * na
    ct = 2 * LANES
    kk = _pick(nc, 16, 1)
    BB = _fine_per_step(nc)
    YCP = BB + 8
    sa, sb = NF // BB, nc // kk
    zo, go = z_off // LANES, gate_off // ct
    ia = lambda s: jnp.minimum(s, sa - 1)
    ib = lambda s: jnp.clip(s - sa, 0, sb - 1)
    ic = lambda s: jnp.maximum(s - sa - sb, 0)
    zspec = lambda h: pl.BlockSpec((None, 2, na, BB, LANES), lambda p, jc, s: (p, 0, 0, ia(s), zo + 2 * jc + h))
    hspec = pl.BlockSpec((None, kk, NF, ct), lambda p, jc, s: (order, ib(s), 0, jc))
    gspec = pl.BlockSpec((2 * NF, 2 * NF), lambda p, jc, s: (0, 0))
    return pl.pallas_call(
        functools.partial(_lconv_kernel, na=na, nc=nc, kk=kk, sa=sa, sb=sb),
        grid=(P, C // ct, 2 * sa + sb),
        in_specs=[zspec(0), zspec(1),
                  pl.BlockSpec((None, 2, na, BB, ct), lambda p, jc, s: (p, 0, 0, ic(s), go + jc)),
                  pl.BlockSpec((BB, 2 * nc, nc), lambda p, jc, s: (ia(s), 0, 0)),
                  pl.BlockSpec((BB, nc, 2 * nc), lambda p, jc, s: (ic(s), 0, 0)),
                  gspec, gspec, hspec, hspec],
        out_specs=pl.BlockSpec((None, 2, na, BB, ct), lambda p, jc, s: (p, 0, 0, ic(s), jc)),
        out_shape=jax.ShapeDtypeStruct((P, 2, na, NF, C), out_dtype),
        scratch_shapes=[_spec_scratch(nc), pltpu.VMEM((2, 2 * na, YCP, LANES), F32)],
        compiler_params=_params("parallel", "parallel", "arbitrary"),
        name="lconv",
    )(z5, z5, gate5, a_data, c_inv, g_fwd, g_inv, spec[0], spec[1])


def _fspec_kernel(t0_ref, t1_ref, at_ref, gf_ref, ss_ref, bias_ref, hr_ref, hi_ref, y_ref,
                  *, nc, kk, sa):
    s = pl.program_id(2)

    @pl.when(s < sa)
    def _():
        _phase_a((t0_ref, t1_ref), at_ref, y_ref, s, planes=1, na=nc, nc=nc)

    @pl.when(s >= sa)
    def _():
        scale = lax.rsqrt(ss_ref[...] + EPS)
        for k in range(kk):
            z = _dot(gf_ref[...], _spec_load(y_ref, _k1_rows(s - sa, kk, k)))
            hr_ref[k] = z[:NF] * scale + bias_ref[...]
            hi_ref[k] = z[NF:] * scale


def _fspec_call(two, ss, bias, tabs, nc):
    _, a_filt, g_fwd, _, _ = tabs
    C = two.shape[-1]
    ct = 2 * LANES
    kk = _pick(nc, 16, 1)
    BB = _fine_per_step(nc)
    sa, sb = NF // BB, nc // kk
    two5 = two.reshape(2, 1, nc, NF, C)
    ia = lambda s: jnp.minimum(s, sa - 1)
    ib = lambda s: jnp.maximum(s - sa, 0)
    tspec = lambda h: pl.BlockSpec((None, 1, nc, BB, LANES), lambda o, jc, s: (o, 0, 0, ia(s), 2 * jc + h))
    vspec = pl.BlockSpec((None, 1, ct), lambda o, jc, s: (o, 0, jc))
    hspec = pl.BlockSpec((None, kk, NF, ct), lambda o, jc, s: (o, ib(s), 0, jc))
    out = jax.ShapeDtypeStruct((2, nc, NF, C), F32)
    return pl.pallas_call(
        functools.partial(_fspec_kernel, nc=nc, kk=kk, sa=sa),
        grid=(2, C // ct, sa + sb),
        in_specs=[tspec(0), tspec(1),
                  pl.BlockSpec((BB, 2 * nc, nc), lambda o, jc, s: (ia(s), 0, 0)),
                  pl.BlockSpec((2 * NF, 2 * NF), lambda o, jc, s: (0, 0)),
                  vspec, vspec],
        out_specs=[hspec, hspec],
        out_shape=[out, out],
        scratch_shapes=[_spec_scratch(nc)],
        compiler_params=_params("parallel", "parallel", "arbitrary"),
        name="fspec",
    )(two5, two5, a_filt, g_fwd, ss, bias)


def _sigmoid(x):
    return 0.5 * (jnp.tanh(0.5 * x) + 1.0)


def _merge_kernel(h_ref, a_ref, b_ref, x_ref, wga_ref, wgb_ref, wa_ref, wb_ref, wo_ref, o_ref,
                  m0_ref, m1_ref, acc_ref, *, nj):
    q = pl.program_id(0)
    first = (jnp.maximum(q - 1, 0) % nj) == 0

    @pl.when(q == 0)
    def _():
        m1_ref[...] = jnp.zeros_like(m1_ref)
        acc_ref[...] = jnp.zeros_like(acc_ref)

    def body(m_new_ref, m_old_ref):
        contrib = _dot(m_old_ref[...], wo_ref[...])
        acc = jnp.where(first, x_ref[...], acc_ref[...]) + contrib
        acc_ref[...] = acc
        o_ref[...] = acc
        h = h_ref[...]
        ga = _sigmoid(_dot(h, wga_ref[...]))
        gb = _sigmoid(_dot(h, wgb_ref[...]))
        m = ga * _dot(a_ref[...], wa_ref[...]) + gb * _dot(b_ref[...], wb_ref[...])
        m_new_ref[...] = m.astype(m_new_ref.dtype)

    @pl.when(q % 2 == 0)
    def _():
        body(m0_ref, m1_ref)

    @pl.when(q % 2 == 1)
    def _():
        body(m1_ref, m0_ref)


def _merge_call(h, a, b, x2, w_ga, w_gb, wa, wb, wo):
    T, D = x2.shape
    Dp, C = a.shape[1], b.shape[1]
    tm = _pick(T, 512)
    tn = _pick(D, 512, LANES)
    nj = D // tn
    Q = (T // tm) * nj
    cur = lambda q: jnp.minimum(q, Q - 1)
    pend = lambda q: jnp.maximum(q - 1, 0)
    row = lambda f: (lambda q: (f(q) // nj, 0))
    col = lambda q: (0, cur(q) % nj)
    return pl.pallas_call(
        functools.partial(_merge_kernel, nj=nj),
        grid=(Q + 1,),
        in_specs=[pl.BlockSpec((tm, D), row(cur)),
                  pl.BlockSpec((tm, Dp), row(cur)),
                  pl.BlockSpec((tm, C), row(cur)),
                  pl.BlockSpec((tm, D), row(pend)),
                  pl.BlockSpec((D, tn), col),
                  pl.BlockSpec((D, tn), col),
                  pl.BlockSpec((Dp, tn), col),
                  pl.BlockSpec((C, tn), col),
                  pl.BlockSpec((tn, D), lambda q: (pend(q) % nj, 0))],
        out_specs=pl.BlockSpec((tm, D), row(pend)),
        out_shape=jax.ShapeDtypeStruct((T, D), F32),
        scratch_shapes=[pltpu.VMEM((tm, tn), MXU_DTYPE), pltpu.VMEM((tm, tn), MXU_DTYPE),
                        pltpu.VMEM((tm, D), F32)],
        compiler_params=_params("arbitrary"),
        name="merge",
    )(h, a, b, x2, w_ga, w_gb, wa, wb, wo)


def _ffn_kernel(xc_ref, xp_ref, g_ref, wg_ref, wu_ref, wd_ref, gf_ref, o_ref,
                h_ref, acc_ref, a0_ref, a1_ref, *, nj, Q):
    q = pl.program_id(0)
    jp = jnp.maximum(q - 1, 0) % nj

    @pl.when((jnp.minimum(q, Q - 1) % nj == 0) & (q < Q))
    def _():
        h_ref[...] = (_rms(xc_ref[...]) * g_ref[...]).astype(h_ref.dtype)

    @pl.when(q == 0)
    def _():
        a1_ref[...] = jnp.zeros_like(a1_ref)
        acc_ref[...] = jnp.zeros_like(acc_ref)

    def body(a_new_ref, a_old_ref):
        down = _dot(a_old_ref[...], wd_ref[...])
        acc_ref[...] = jnp.where(jp == 0, xp_ref[...], acc_ref[...]) + down
        h = h_ref[...]
        gate = _dot(h, wg_ref[...])
        act = gate * _sigmoid(gate) * _dot(h, wu_ref[...])
        a_new_ref[...] = act.astype(a_new_ref.dtype)

    @pl.when(q % 2 == 0)
    def _():
        body(a0_ref, a1_ref)

    @pl.when(q % 2 == 1)
    def _():
        body(a1_ref, a0_ref)

    @pl.when(jp == nj - 1)
    def _():
        o_ref[...] = _rms(acc_ref[...]) * gf_ref[...]


def _ffn_call(x2, g, wg, wu, wd, gf):
    T, D = x2.shape
    Dff = wg.shape[1]
    tm = _pick(T, 512)
    tf = _pick(Dff, 512, LANES)
    nj = Dff // tf
    Q = (T // tm) * nj
    cur = lambda q: jnp.minimum(q, Q - 1)
    pend = lambda q: jnp.maximum(q - 1, 0)
    const = lambda q: (0, 0)
    return pl.pallas_call(
        functools.partial(_ffn_kernel, nj=nj, Q=Q),
        grid=(Q + 1,),
        in_specs=[pl.BlockSpec((tm, D), lambda q: (cur(q) // nj, 0)),
                  pl.BlockSpec((tm, D), lambda q: (pend(q) // nj, 0)),
                  pl.BlockSpec((1, D), const),
                  pl.BlockSpec((D, tf), lambda q: (0, cur(q) % nj)),
                  pl.BlockSpec((D, tf), lambda q: (0, cur(q) % nj)),
                  pl.BlockSpec((tf, D), lambda q: (pend(q) % nj, 0)),
                  pl.BlockSpec((1, D), const)],
        out_specs=pl.BlockSpec((tm, D), lambda q: (pend(q) // nj, 0)),
        out_shape=jax.ShapeDtypeStruct((T, D), F32),
        scratch_shapes=[pltpu.VMEM((tm, D), MXU_DTYPE), pltpu.VMEM((tm, D), F32),
                        pltpu.VMEM((tm, tf), MXU_DTYPE), pltpu.VMEM((tm, tf), MXU_DTYPE)],
        compiler_params=_params("arbitrary"),
        name="ffn",
    )(x2, x2, g.reshape(1, D), wg, wu, wd, gf.reshape(1, D))


def _trunk(x, w):
    B, L, D = x.shape
    T = B * L
    C = w["hyena_bias"].shape[1]
    assert B % 2 == 0 and L % NF == 0 and C % (2 * LANES) == 0 and w["hyena_bias"].shape[0] == 2
    na = L // NF
    nc = 2 * na
    P = B // 2
    x2 = x.reshape(T, D)

    h = _norm_call(x2, w["g_mix"])
    a = _pool_call(h, w["w_pool"], w["pool_w"], w["pool_scale"], L)
    uc = _hyproj_call(h, w["w_hy"], w["conv_w"], w["conv_b"], L, C)
    uc5 = uc.reshape(P, 2, na, NF, 3 * C)

    tabs = _fft_tables(nc)
    two, ss = _filter_call(L, C, *w["filt"])
    spec = _fspec_call(two, ss, w["hyena_bias"].reshape(2, 1, C), tabs, nc)

    z1 = _lconv_call(uc5, 0, uc5, C, spec, 0, tabs, na, F32)
    b = _lconv_call(z1, 0, uc5, 2 * C, spec, 1, tabs, na, MXU_DTYPE).reshape(T, C)

    xn = _merge_call(h, a, b, x2, w["w_ga"], w["w_gb"], w["wa"], w["wb"], w["wo"])
    y = _ffn_call(xn, w["g_ffn"], w["wg"], w["wu"], w["wd"], w["g_final"])
    return y.reshape(B, L, D)


def kernel(x_prompt, x_sample, g_mix, w_in, pool_w, pool_scale, conv_w, conv_b, filt_w1, filt_b1, filt_freq1, filt_w2, filt_b2, filt_freq2, filt_w3, hyena_bias, w_branch_a, w_branch_b, w_out, g_ffn, w_gate, w_up, w_down, g_final):
    assert g_mix.shape[0] == 1, "depth-1 block"
    D = x_prompt.shape[-1]
    Dp = pool_scale.shape[1]
    C = hyena_bias.shape[2]
    s1, s2, s3 = Dp, Dp + 3 * C, Dp + 3 * C + D
    cast = lambda t: t.astype(MXU_DTYPE)
    w = {
        "g_mix": g_mix[0], "w_pool": cast(w_in[0][:, :s1]), "w_hy": cast(w_in[0][:, s1:s2]),
        "w_ga": cast(w_in[0][:, s2:s3]), "w_gb": cast(w_in[0][:, s3:]),
        "pool_w": cast(pool_w[0]), "pool_scale": pool_scale[0],
        "conv_w": conv_w[0], "conv_b": conv_b[0],
        "filt": (filt_w1[0], filt_b1[0], filt_freq1[0], filt_w2[0], filt_b2[0], filt_freq2[0], filt_w3[0]),
        "hyena_bias": hyena_bias[0],
        "wa": cast(w_branch_a[0]), "wb": cast(w_branch_b[0]), "wo": cast(w_out[0]),
        "g_ffn": g_ffn[0], "wg": cast(w_gate[0]), "wu": cast(w_up[0]), "wd": cast(w_down[0]),
        "g_final": g_final,
    }
    return (_trunk(x_prompt, w), _trunk(x_sample, w))
```

```python
import functools
import math

import jax
import jax.numpy as jnp
from jax import lax
from jax.experimental import pallas as pl
from jax.experimental.pallas import tpu as pltpu

F32 = jnp.float32
MXU_DTYPE = jnp.bfloat16

EPS = 1e-6
POOL_WINDOWS = (2, 4, 8, 16)
DECAY_TARGET = 1e-2
FAST_DECAY_PCT = 0.3
SLOW_DECAY_PCT = 1.5

LANES = 128
NF = 128
HALO = 16
VMEM_LIMIT = 56 * 1024 * 1024


def _pick(n, pref, mult=8):
    best = None
    for d in range(mult, min(n, pref) + 1, mult):
        if n % d == 0:
            best = d
    return best if best is not None else n


def _params(*sem):
    return pltpu.CompilerParams(dimension_semantics=sem, vmem_limit_bytes=VMEM_LIMIT)


def _dot(a, b, **kw):
    return jnp.dot(a, b, preferred_element_type=F32, **kw)


def _rms(x):
    return x * lax.rsqrt(jnp.mean(x * x, axis=-1, keepdims=True) + EPS)


def _halo_specs(tm, T, D, tile_of):
    r = tm // HALO
    last = T // HALO - 1
    return [
        pl.BlockSpec((HALO, D), lambda *g: (jnp.maximum(tile_of(*g) * r - 1, 0), 0)),
        pl.BlockSpec((tm, D), lambda *g: (tile_of(*g), 0)),
        pl.BlockSpec((HALO, D), lambda *g: (jnp.minimum((tile_of(*g) + 1) * r, last), 0)),
    ]


def _fill_hcat(hcat_ref, hp_ref, hm_ref, hn_ref, tm):
    hcat_ref[pl.ds(0, HALO), :] = hp_ref[...]
    hcat_ref[pl.ds(HALO, tm), :] = hm_ref[...]
    hcat_ref[pl.ds(HALO + tm, HALO), :] = hn_ref[...]


def _pool_chunks(tm):
    rows = tm + 2 * HALO
    ch = min(tm, 256)
    kw = min(rows, 512)
    return ch, kw, [(c * ch, min(c * ch, rows - kw)) for c in range(tm // ch)]


def _pool_kernel(xp_ref, xm_ref, xn_ref, g_ref, w_ref, pw_ref, ps_ref, o_ref, h_ref, hcat_ref, band_ref,
                 *, tm, L):
    rows = tm + 2 * HALO
    gw = pw_ref.shape[-1]
    ch, kw, chunks = _pool_chunks(tm)

    @pl.when(pl.program_id(0) == 0)
    def _():
        r = lax.broadcasted_iota(jnp.int32, (ch, kw), 0)
        s = lax.broadcasted_iota(jnp.int32, (ch, kw), 1)
        for g, w in enumerate(POOL_WINDOWS):
            for c, (r0, s0) in enumerate(chunks):
                d = (s + (s0 - HALO)) - (r + r0)
                band_ref[g, c] = jnp.where((d >= -(w // 2)) & (d < w - w // 2), 1.0, 0.0).astype(MXU_DTYPE)

    for x_ref, row0, n in ((xp_ref, 0, HALO), (xm_ref, HALO, tm), (xn_ref, HALO + tm, HALO)):
        hcat_ref[pl.ds(row0, n), :] = (_rms(x_ref[...]) * g_ref[...]).astype(hcat_ref.dtype)
    h_ref[...] = hcat_ref[pl.ds(HALO, tm), :]
    p = _dot(hcat_ref[...], w_ref[...])
    t0 = (pl.program_id(0) * tm) % L
    tp = t0 - HALO + lax.broadcasted_iota(jnp.int32, (rows, 1), 0)
    p = jnp.where((tp >= 0) & (tp < L), p, 0.0)
    p_hi = p.astype(MXU_DTYPE)
    p_lo = (p - p_hi.astype(F32)).astype(MXU_DTYPE)
    tc = t0 + lax.broadcasted_iota(jnp.int32, (tm, 1), 0)
    for g, w in enumerate(POOL_WINDOWS):
        lo_off, hi_off = w // 2, w - w // 2
        sl = slice(g * gw, (g + 1) * gw)
        wsum = jnp.concatenate(
            [_dot(band_ref[g, c], p_hi[s0:s0 + kw, sl]) + _dot(band_ref[g, c], p_lo[s0:s0 + kw, sl])
             for c, (_, s0) in enumerate(chunks)], axis=0)
        cnt = (jnp.minimum(tc + hi_off, L) - jnp.maximum(tc - lo_off, 0)).astype(F32)
        pooled = wsum / cnt - p[HALO:HALO + tm, sl]
        mixed = _dot(pooled.astype(MXU_DTYPE), pw_ref[g]) * ps_ref[:, sl]
        o_ref[:, sl] = mixed.astype(o_ref.dtype)


def _pool_call(x2, g, w_pool, pool_w, pool_scale, L):
    T, D = x2.shape
    Dp = w_pool.shape[1]
    G, gw, _ = pool_w.shape
    tm = _pick(L, 512, HALO)
    ch, kw, chunks = _pool_chunks(tm)
    return pl.pallas_call(
        functools.partial(_pool_kernel, tm=tm, L=L),
        grid=(T // tm,),
        in_specs=_halo_specs(tm, T, D, lambda i: i) + [
            pl.BlockSpec((1, D), lambda i: (0, 0)),
            pl.BlockSpec((D, Dp), lambda i: (0, 0)),
            pl.BlockSpec((G, gw, gw), lambda i: (0, 0, 0)),
            pl.BlockSpec((1, Dp), lambda i: (0, 0)),
        ],
        out_specs=[pl.BlockSpec((tm, Dp), lambda i: (i, 0)),
                   pl.BlockSpec((tm, D), lambda i: (i, 0))],
        out_shape=[jax.ShapeDtypeStruct((T, Dp), MXU_DTYPE),
                   jax.ShapeDtypeStruct((T, D), MXU_DTYPE)],
        scratch_shapes=[pltpu.VMEM((tm + 2 * HALO, D), MXU_DTYPE),
                        pltpu.VMEM((len(POOL_WINDOWS), len(chunks), ch, kw), MXU_DTYPE)],
        compiler_params=_params("arbitrary"),
        name="pool",
    )(x2, x2, x2, g.reshape(1, D), w_pool, pool_w, pool_scale.reshape(1, Dp))


def _hyproj_kernel(hp_ref, hm_ref, hn_ref, w_ref, cw_ref, cb_ref, o_ref, hcat_ref, *, tm, L):
    rows = tm + 2 * HALO

    @pl.when(pl.program_id(1) == 0)
    def _():
        _fill_hcat(hcat_ref, hp_ref, hm_ref, hn_ref, tm)

    t0 = (pl.program_id(0) * tm) % L
    tp = t0 - HALO + lax.broadcasted_iota(jnp.int32, (rows, 1), 0)
    p = jnp.where((tp >= 0) & (tp < L), _dot(hcat_ref[...], w_ref[...]), 0.0)
    prev = pltpu.roll(p, 1, 0)
    nxt = pltpu.roll(p, rows - 1, 0)
    y = prev * cw_ref[0:1, :] + p * cw_ref[1:2, :] + nxt * cw_ref[2:3, :] + cb_ref[...]
    o_ref[...] = y[HALO:HALO + tm]


def _hyproj_call(h, w_hy, conv_w, conv_b, L, C):
    T, D = h.shape
    n3 = w_hy.shape[1]
    tn = C if C % LANES == 0 else n3
    tm = _pick(L, 512, HALO)
    return pl.pallas_call(
        functools.partial(_hyproj_kernel, tm=tm, L=L),
        grid=(T // tm, n3 // tn),
        in_specs=_halo_specs(tm, T, D, lambda i, j: i) + [
            pl.BlockSpec((D, tn), lambda i, j: (0, j)),
            pl.BlockSpec((conv_w.shape[0], tn), lambda i, j: (0, j)),
            pl.BlockSpec((1, tn), lambda i, j: (0, j)),
        ],
        out_specs=pl.BlockSpec((tm, tn), lambda i, j: (i, j)),
        out_shape=jax.ShapeDtypeStruct((T, n3), F32),
        scratch_shapes=[pltpu.VMEM((tm + 2 * HALO, D), MXU_DTYPE)],
        compiler_params=_params("parallel", "arbitrary"),
        name="hyproj",
    )(h, h, h, w_hy, conv_w, conv_b.reshape(1, n3))


def _filter_kernel(w1_ref, b1_ref, f1_ref, w2_ref, b2_ref, f2_ref, w3_ref, two_ref, ss_ref,
                   *, L, C, fb, tmf):
    i = pl.program_id(0)
    N = 2 * L
    hp = lax.Precision.HIGHEST
    m = i * tmf + lax.broadcasted_iota(jnp.int32, (tmf, 1), 0)
    tf = jnp.where(m < L, m, N - m).astype(F32)
    tt = tf / (L - 1)
    wt = (2.0 * math.pi) * tf / L
    lane = lax.broadcasted_iota(jnp.int32, (1, LANES), 1)
    bidx = jnp.where(lane <= fb, lane - 1, lane - 1 - fb).astype(F32)
    bands = 1e-4 + bidx * ((fb - 1 - 1e-4) / (fb - 1))
    ang = bands * wt
    z = jnp.where(lane == 0, tt,
                  jnp.where(lane <= fb, jnp.cos(ang),
                            jnp.where(lane <= 2 * fb, -jnp.sin(ang), 0.0)))
    h = jnp.sin(f1_ref[...] * (_dot(z, w1_ref[...], precision=hp) + b1_ref[...]))
    h = jnp.sin(f2_ref[...] * (_dot(h, w2_ref[...], precision=hp) + b2_ref[...]))
    h_hi = h.astype(MXU_DTYPE)
    h_lo = (h - h_hi.astype(F32)).astype(MXU_DTYPE)
    hk = jnp.concatenate([h_hi, h_hi, h_lo], axis=1)
    max_decay = math.log(DECAY_TARGET) / FAST_DECAY_PCT
    min_decay = math.log(DECAY_TARGET) / SLOW_DECAY_PCT
    cl = lax.broadcasted_iota(jnp.int32, (1, C), 1).astype(F32)
    deltas = jnp.abs(min_decay + cl * ((max_decay - min_decay) / (C - 1)))
    decay = jnp.where(m == L, 0.0, jnp.exp(-tt * deltas))
    decay2 = jnp.concatenate([decay, decay], axis=1)

    def emit(two):
        two_ref[0] = two[:, :C]
        two_ref[1] = two[:, C:]
        ss_ref[...] += jnp.sum(two * two, axis=0, keepdims=True)

    @pl.when(i == 0)
    def _():
        ss_ref[...] = jnp.zeros_like(ss_ref)
        lag0 = jnp.where(m == 0, 1.0, 0.0)
        emit(decay2 * (_dot(hk, w3_ref[0]) + lag0 * _dot(hk, w3_ref[1])))

    @pl.when(i != 0)
    def _():
        emit(decay2 * _dot(hk, w3_ref[(i * tmf >= L).astype(jnp.int32)]))


def _filter_call(L, C, w1, b1, f1, w2, b2, f2, w3):
    N = 2 * L
    fe, fh = w1.shape
    fb = (fe - 1) // 2
    assert fe <= LANES
    w1p = jnp.zeros((LANES, fh), F32).at[:fe].set(w1)
    tmf = _pick(L, 512)
    w3d = jnp.transpose(w3.reshape(fh, 2, 2, C), (2, 0, 1, 3)).reshape(2, fh, 2 * C)
    w3_hi = w3d.astype(MXU_DTYPE)
    w3_lo = (w3d - w3_hi.astype(F32)).astype(MXU_DTYPE)
    w3k = jnp.concatenate([w3_hi, w3_lo, w3_hi], axis=1)
    full = lambda a: pl.BlockSpec(a.shape, lambda i: (0,) * a.ndim)
    args = (w1p, b1.reshape(1, fh), f1.reshape(1, fh), w2, b2.reshape(1, fh), f2.reshape(1, fh), w3k)
    two, ss = pl.pallas_call(
        functools.partial(_filter_kernel, L=L, C=C, fb=fb, tmf=tmf),
        grid=(N // tmf,),
        in_specs=[full(a) for a in args],
        out_specs=[pl.BlockSpec((2, tmf, C), lambda i: (0, i, 0)),
                   pl.BlockSpec((1, 2 * C), lambda i: (0, 0))],
        out_shape=[jax.ShapeDtypeStruct((2, N, C), F32),
                   jax.ShapeDtypeStruct((1, 2 * C), F32)],
        compiler_params=_params("arbitrary"),
        name="filter",
    )(*args)
    return two, ss.reshape(2, 1, C)


def _stack(mr, mi):
    return jnp.concatenate([jnp.concatenate([mr, -mi], axis=-1),
                            jnp.concatenate([mi, mr], axis=-1)], axis=-2)


def _il_rows(m):
    n = m.shape[-2] // 2
    return jnp.stack([m[..., :n, :], m[..., n:, :]], axis=-2).reshape(m.shape)


def _il_cols(m):
    return jnp.swapaxes(_il_rows(jnp.swapaxes(m, -1, -2)), -1, -2)


def _fft_tables(nc):
    N = nc * NF
    na = nc // 2
    i32 = jnp.int32
    k1 = jnp.arange(nc, dtype=i32)
    b = jnp.arange(NF, dtype=i32)
    e = (k1[None, :, None] * k1[None, None, :] * NF + b[:, None, None] * k1[None, :, None]) % N
    ang = (-2.0 * math.pi / N) * e.astype(F32)
    ar, ai = jnp.cos(ang), jnp.sin(ang)
    a_data = _il_rows(_stack(ar[:, :, :na], ai[:, :, :na]))
    a_filt = _il_rows(jnp.concatenate([ar, ai], axis=1))
    cr, ci = jnp.swapaxes(ar, 1, 2)[:, :na] / nc, -jnp.swapaxes(ai, 1, 2)[:, :na] / nc
    c_inv = _il_cols(_stack(cr, ci))
    ang = (-2.0 * math.pi / NF) * ((b[:, None] * b[None, :]) % NF).astype(F32)
    gr, gi = jnp.cos(ang), jnp.sin(ang)
    g_fwd = _il_cols(_stack(gr, gi))
    g_inv = _il_rows(_stack(gr.T / NF, -gi.T / NF))
    cast = lambda t: t.astype(MXU_DTYPE)
    return cast(a_data), cast(a_filt), cast(g_fwd), cast(g_inv), cast(c_inv)


NFP = NF + 8
U32 = jnp.uint32
PACKED = True


def _fine_per_step(nc):
    return max(8, min(32, (16 * 128) // nc))


def _rows2d(ref):
    return ref.reshape(math.prod(ref.shape[:-1]), ref.shape[-1])


def _ld_rows(ref, start, n, stride):
    return _rows2d(ref)[pl.ds(start, n, stride=stride), :]


def _st_rows(ref, start, n, stride, val):
    _rows2d(ref)[pl.ds(start, n, stride=stride), :] = val


def _spec_scratch(nc):
    if PACKED:
        return pltpu.VMEM((2, 1, nc * NFP, LANES), U32)
    return pltpu.VMEM((2, 2, nc * NFP, LANES), F32)


def _spec_store(y_ref, rows, v):
    if PACKED:
        w = pltpu.bitcast(v.astype(jnp.bfloat16), U32)
        y_ref[0, 0, rows, :] = w[:, :LANES]
        y_ref[1, 0, rows, :] = w[:, LANES:]
    else:
        for half in range(2):
            lanes = slice(half * LANES, (half + 1) * LANES)
            y_ref[half, 0, rows, :] = v[0::2, lanes]
            y_ref[half, 1, rows, :] = v[1::2, lanes]


def _spec_load(y_ref, rows):
    if PACKED:
        w = jnp.concatenate([y_ref[0, 0, rows, :], y_ref[1, 0, rows, :]], axis=1)
        return pltpu.bitcast(w, jnp.bfloat16)
    re = jnp.concatenate([y_ref[0, 0, rows, :], y_ref[1, 0, rows, :]], axis=1)
    im = jnp.concatenate([y_ref[0, 1, rows, :], y_ref[1, 1, rows, :]], axis=1)
    return jnp.stack([re, im], axis=1).reshape(2 * re.shape[0], re.shape[1]).astype(MXU_DTYPE)


def _phase_a(z_refs, at_ref, y_ref, step, *, planes, na, nc):
    bb = at_ref.shape[0]
    for b in range(bb):
        x = jnp.concatenate(
            [jnp.concatenate([_ld_rows(z, p * na * bb + b, na, bb) for p in range(planes)], axis=0)
             for z in z_refs], axis=1)
        r = _dot(at_ref[b], x.astype(MXU_DTYPE))
        _spec_store(y_ref, pl.ds(step * bb + b, nc, stride=NFP), r)


def _k1_rows(step, kk, k):
    return pl.ds(pl.multiple_of((step * kk + k) * NFP, 8), NF)


def _lconv_kernel(z0_ref, z1_ref, gate_ref, at_ref, ct_ref, gf_ref, gi_ref, hr_ref, hi_ref,
                  o_ref, y_ref, yc_ref, *, na, nc, kk, sa, sb):
    s = pl.program_id(2)

    @pl.when(s < sa)
    def _():
        _phase_a((z0_ref, z1_ref), at_ref, y_ref, s, planes=2, na=na, nc=nc)

    @pl.when((s >= sa) & (s < sa + sb))
    def _():
        for k in range(kk):
            rows = _k1_rows(s - sa, kk, k)
            z = _dot(gf_ref[...], _spec_load(y_ref, rows))
            zr, zi = z[:NF], z[NF:]
            hr, hi = hr_ref[k].astype(F32), hi_ref[k].astype(F32)
            x2 = jnp.concatenate([zr * hr - zi * hi, zr * hi + zi * hr], axis=0).astype(MXU_DTYPE)
            u = _dot(gi_ref[...], x2)
            _spec_store(y_ref, rows, u)

    @pl.when(s >= sa + sb)
    def _():
        step = s - sa - sb
        bb, ycp = ct_ref.shape[0], yc_ref.shape[2]
        for b in range(bb):
            x = _spec_load(y_ref, pl.ds(step * bb + b, nc, stride=NFP))
            r = _dot(ct_ref[b], x)
            for half in range(2):
                _st_rows(yc_ref.at[half], b, 2 * na, ycp, r[:, half * LANES:(half + 1) * LANES])
        for half in range(2):
            lanes = slice(half * LANES, (half + 1) * LANES)
            y = yc_ref[half, :, :bb, :].reshape(2, na, bb, LANES)
            o_ref[:, :, :, lanes] = (gate_ref[:, :, :, lanes] * y).astype(o_ref.dtype)


def _lconv_call(z5, z_off, gate5, gate_off, spec, order, tabs, na, out_dtype):
    a_data, _, g_fwd, g_inv, c_inv = tabs
    P = z5.shape[0]
    C = spec[0].shape[-1]
    nc = 2 * na
    ct = 2 * LANES
    kk = _pick(nc, 16, 1)
    BB = _fine_per_step(nc)
    YCP = BB + 8
    sa, sb = NF // BB, nc // kk
    zo, go = z_off // LANES, gate_off // ct
    ia = lambda s: jnp.minimum(s, sa - 1)
    ib = lambda s: jnp.clip(s - sa, 0, sb - 1)
    ic = lambda s: jnp.maximum(s - sa - sb, 0)
    zspec = lambda h: pl.BlockSpec((None, 2, na, BB, LANES), lambda p, jc, s: (p, 0, 0, ia(s), zo + 2 * jc + h))
    hspec = pl.BlockSpec((None, kk, NF, ct), lambda p, jc, s: (order, ib(s), 0, jc))
    gspec = pl.BlockSpec((2 * NF, 2 * NF), lambda p, jc, s: (0, 0))
    return pl.pallas_call(
        functools.partial(_lconv_kernel, na=na, nc=nc, kk=kk, sa=sa, sb=sb),
        grid=(P, C // ct, 2 * sa + sb),
        in_specs=[zspec(0), zspec(1),
                  pl.BlockSpec((None, 2, na, BB, ct), lambda p, jc, s: (p, 0, 0, ic(s), go + jc)),
                  pl.BlockSpec((BB, 2 * nc, nc), lambda p, jc, s: (ia(s), 0, 0)),
                  pl.BlockSpec((BB, nc, 2 * nc), lambda p, jc, s: (ic(s), 0, 0)),
                  gspec, gspec, hspec, hspec],
        out_specs=pl.BlockSpec((None, 2, na, BB, ct), lambda p, jc, s: (p, 0, 0, ic(s), jc)),
        out_shape=jax.ShapeDtypeStruct((P, 2, na, NF, C), out_dtype),
        scratch_shapes=[_spec_scratch(nc), pltpu.VMEM((2, 2 * na, YCP, LANES), F32)],
        compiler_params=_params("parallel", "parallel", "arbitrary"),
        name="lconv",
    )(z5, z5, gate5, a_data, c_inv, g_fwd, g_inv, spec[0], spec[1])


def _fspec_kernel(t0_ref, t1_ref, at_ref, gf_ref, ss_ref, bias_ref, hr_ref, hi_ref, y_ref,
                  *, nc, kk, sa):
    s = pl.program_id(2)

    @pl.when(s < sa)
    def _():
        _phase_a((t0_ref, t1_ref), at_ref, y_ref, s, planes=1, na=nc, nc=nc)

    @pl.when(s >= sa)
    def _():
        scale = lax.rsqrt(ss_ref[...] + EPS)
        for k in range(kk):
            z = _dot(gf_ref[...], _spec_load(y_ref, _k1_rows(s - sa, kk, k)))
            hr_ref[k] = (z[:NF] * scale + bias_ref[...]).astype(hr_ref.dtype)
            hi_ref[k] = (z[NF:] * scale).astype(hi_ref.dtype)


def _fspec_call(two, ss, bias, tabs, nc):
    _, a_filt, g_fwd, _, _ = tabs
    C = two.shape[-1]
    ct = 2 * LANES
    kk = _pick(nc, 16, 1)
    BB = _fine_per_step(nc)
    sa, sb = NF // BB, nc // kk
    two5 = two.reshape(2, 1, nc, NF, C)
    ia = lambda s: jnp.minimum(s, sa - 1)
    ib = lambda s: jnp.maximum(s - sa, 0)
    tspec = lambda h: pl.BlockSpec((None, 1, nc, BB, LANES), lambda o, jc, s: (o, 0, 0, ia(s), 2 * jc + h))
    vspec = pl.BlockSpec((None, 1, ct), lambda o, jc, s: (o, 0, jc))
    hspec = pl.BlockSpec((None, kk, NF, ct), lambda o, jc, s: (o, ib(s), 0, jc))
    out = jax.ShapeDtypeStruct((2, nc, NF, C), MXU_DTYPE)
    return pl.pallas_call(
        functools.partial(_fspec_kernel, nc=nc, kk=kk, sa=sa),
        grid=(2, C // ct, sa + sb),
        in_specs=[tspec(0), tspec(1),
                  pl.BlockSpec((BB, 2 * nc, nc), lambda o, jc, s: (ia(s), 0, 0)),
                  pl.BlockSpec((2 * NF, 2 * NF), lambda o, jc, s: (0, 0)),
                  vspec, vspec],
        out_specs=[hspec, hspec],
        out_shape=[out, out],
        scratch_shapes=[_spec_scratch(nc)],
        compiler_params=_params("parallel", "parallel", "arbitrary"),
        name="fspec",
    )(two5, two5, a_filt, g_fwd, ss, bias)


def _sigmoid(x):
    return 0.5 * (jnp.tanh(0.5 * x) + 1.0)


def _merge_kernel(h_ref, a_ref, b_ref, x_ref, wga_ref, wgb_ref, wa_ref, wb_ref, wo_ref, o_ref,
                  m0_ref, m1_ref, acc_ref, *, nj):
    q = pl.program_id(0)
    first = (jnp.maximum(q - 1, 0) % nj) == 0

    @pl.when(q == 0)
    def _():
        m1_ref[...] = jnp.zeros_like(m1_ref)
        acc_ref[...] = jnp.zeros_like(acc_ref)

    def body(m_new_ref, m_old_ref):
        contrib = _dot(m_old_ref[...], wo_ref[...])
        acc = jnp.where(first, x_ref[...], acc_ref[...]) + contrib
        acc_ref[...] = acc
        o_ref[...] = acc
        h = h_ref[...]
        ga = _sigmoid(_dot(h, wga_ref[...]))
        gb = _sigmoid(_dot(h, wgb_ref[...]))
        m = ga * _dot(a_ref[...], wa_ref[...]) + gb * _dot(b_ref[...], wb_ref[...])
        m_new_ref[...] = m.astype(m_new_ref.dtype)

    @pl.when(q % 2 == 0)
    def _():
        body(m0_ref, m1_ref)

    @pl.when(q % 2 == 1)
    def _():
        body(m1_ref, m0_ref)


def _merge_call(h, a, b, x2, w_ga, w_gb, wa, wb, wo):
    T, D = x2.shape
    Dp, C = a.shape[1], b.shape[1]
    tm = _pick(T, 512)
    tn = _pick(D, 512, LANES)
    nj = D // tn
    Q = (T // tm) * nj
    cur = lambda q: jnp.minimum(q, Q - 1)
    pend = lambda q: jnp.maximum(q - 1, 0)
    row = lambda f: (lambda q: (f(q) // nj, 0))
    col = lambda q: (0, cur(q) % nj)
    return pl.pallas_call(
        functools.partial(_merge_kernel, nj=nj),
        grid=(Q + 1,),
        in_specs=[pl.BlockSpec((tm, D), row(cur)),
                  pl.BlockSpec((tm, Dp), row(cur)),
                  pl.BlockSpec((tm, C), row(cur)),
                  pl.BlockSpec((tm, D), row(pend)),
                  pl.BlockSpec((D, tn), col),
                  pl.BlockSpec((D, tn), col),
                  pl.BlockSpec((Dp, tn), col),
                  pl.BlockSpec((C, tn), col),
                  pl.BlockSpec((tn, D), lambda q: (pend(q) % nj, 0))],
        out_specs=pl.BlockSpec((tm, D), row(pend)),
        out_shape=jax.ShapeDtypeStruct((T, D), F32),
        scratch_shapes=[pltpu.VMEM((tm, tn), MXU_DTYPE), pltpu.VMEM((tm, tn), MXU_DTYPE),
                        pltpu.VMEM((tm, D), F32)],
        compiler_params=_params("arbitrary"),
        name="merge",
    )(h, a, b, x2, w_ga, w_gb, wa, wb, wo)


def _ffn_kernel(x_ref, g_ref, wg_ref, wu_ref, wd_ref, gf_ref, o_ref, h_ref, acc_ref):
    j = pl.program_id(1)

    @pl.when(j == 0)
    def _():
        x = x_ref[...]
        h_ref[...] = (_rms(x) * g_ref[...]).astype(h_ref.dtype)
        acc_ref[...] = x

    h = h_ref[...]
    act = jax.nn.silu(_dot(h, wg_ref[...])) * _dot(h, wu_ref[...])
    acc_ref[...] += _dot(act.astype(MXU_DTYPE), wd_ref[...])

    @pl.when(j == pl.num_programs(1) - 1)
    def _():
        o_ref[...] = _rms(acc_ref[...]) * gf_ref[...]


def _ffn_call(x2, g, wg, wu, wd, gf):
    T, D = x2.shape
    Dff = wg.shape[1]
    tm = _pick(T, 512)
    tf = _pick(Dff, 512, LANES)
    return pl.pallas_call(
        _ffn_kernel,
        grid=(T // tm, Dff // tf),
        in_specs=[pl.BlockSpec((tm, D), lambda i, j: (i, 0)),
                  pl.BlockSpec((1, D), lambda i, j: (0, 0)),
                  pl.BlockSpec((D, tf), lambda i, j: (0, j)),
                  pl.BlockSpec((D, tf), lambda i, j: (0, j)),
                  pl.BlockSpec((tf, D), lambda i, j: (j, 0)),
                  pl.BlockSpec((1, D), lambda i, j: (0, 0))],
        out_specs=pl.BlockSpec((tm, D), lambda i, j: (i, 0)),
        out_shape=jax.ShapeDtypeStruct((T, D), F32),
        scratch_shapes=[pltpu.VMEM((tm, D), MXU_DTYPE), pltpu.VMEM((tm, D), F32)],
        compiler_params=_params("parallel", "arbitrary"),
        name="ffn",
    )(x2, g.reshape(1, D), wg, wu, wd, gf.reshape(1, D))


def _trunk(x, w):
    B, L, D = x.shape
    T = B * L
    C = w["hyena_bias"].shape[1]
    assert B % 2 == 0 and L % NF == 0 and C % (2 * LANES) == 0 and w["hyena_bias"].shape[0] == 2
    na = L // NF
    nc = 2 * na
    P = B // 2
    x2 = x.reshape(T, D)

    a, h = _pool_call(x2, w["g_mix"], w["w_pool"], w["pool_w"], w["pool_scale"], L)
    uc = _hyproj_call(h, w["w_hy"], w["conv_w"], w["conv_b"], L, C)
    uc5 = uc.reshape(P, 2, na, NF, 3 * C)

    tabs = _fft_tables(nc)
    two, ss = _filter_call(L, C, *w["filt"])
    spec = _fspec_call(two, ss, w["hyena_bias"].reshape(2, 1, C), tabs, nc)

    z1 = _lconv_call(uc5, 0, uc5, C, spec, 0, tabs, na, F32)
    b = _lconv_call(z1, 0, uc5, 2 * C, spec, 1, tabs, na, MXU_DTYPE).reshape(T, C)

    xn = _merge_call(h, a, b, x2, w["w_ga"], w["w_gb"], w["wa"], w["wb"], w["wo"])
    y = _ffn_call(xn, w["g_ffn"], w["wg"], w["wu"], w["wd"], w["g_final"])
    return y.reshape(B, L, D)


def kernel(x_prompt, x_sample, g_mix, w_in, pool_w, pool_scale, conv_w, conv_b, filt_w1, filt_b1, filt_freq1, filt_w2, filt_b2, filt_freq2, filt_w3, hyena_bias, w_branch_a, w_branch_b, w_out, g_ffn, w_gate, w_up, w_down, g_final):
    assert g_mix.shape[0] == 1, "depth-1 block"
    D = x_prompt.shape[-1]
    Dp = pool_scale.shape[1]
    C = hyena_bias.shape[2]
    s1, s2, s3 = Dp, Dp + 3 * C, Dp + 3 * C + D
    cast = lambda t: t.astype(MXU_DTYPE)
    w = {
        "g_mix": g_mix[0], "w_pool": cast(w_in[0][:, :s1]), "w_hy": cast(w_in[0][:, s1:s2]),
        "w_ga": cast(w_in[0][:, s2:s3]), "w_gb": cast(w_in[0][:, s3:]),
        "pool_w": cast(pool_w[0]), "pool_scale": pool_scale[0],
        "conv_w": conv_w[0], "conv_b": conv_b[0],
        "filt": (filt_w1[0], filt_b1[0], filt_freq1[0], filt_w2[0], filt_b2[0], filt_freq2[0], filt_w3[0]),
        "hyena_bias": hyena_bias[0],
        "wa": cast(w_branch_a[0]), "wb": cast(w_branch_b[0]), "wo": cast(w_out[0]),
        "g_ffn": g_ffn[0], "wg": cast(w_gate[0]), "wu": cast(w_up[0]), "wd": cast(w_down[0]),
        "g_final": g_final,
    }
    return (_trunk(x_prompt, w), _trunk(x_sample, w))
```

```python
import functools
import math

import jax
import jax.numpy as jnp
from jax import lax
from jax.experimental import pallas as pl
from jax.experimental.pallas import tpu as pltpu

F32 = jnp.float32
MXU_DTYPE = jnp.bfloat16

EPS = 1e-6
POOL_WINDOWS = (2, 4, 8, 16)
DECAY_TARGET = 1e-2
FAST_DECAY_PCT = 0.3
SLOW_DECAY_PCT = 1.5

LANES = 128
NF = 128
HALO = 16
VMEM_LIMIT = 56 * 1024 * 1024


def _pick(n, pref, mult=8):
    best = None
    for d in range(mult, min(n, pref) + 1, mult):
        if n % d == 0:
            best = d
    return best if best is not None else n


def _params(*sem):
    return pltpu.CompilerParams(dimension_semantics=sem, vmem_limit_bytes=VMEM_LIMIT)


def _dot(a, b, **kw):
    return jnp.dot(a, b, preferred_element_type=F32, **kw)


def _rms(x):
    return x * lax.rsqrt(jnp.mean(x * x, axis=-1, keepdims=True) + EPS)


def _halo_specs(tm, T, D, tile_of):
    r = tm // HALO
    last = T // HALO - 1
    return [
        pl.BlockSpec((HALO, D), lambda *g: (jnp.maximum(tile_of(*g) * r - 1, 0), 0)),
        pl.BlockSpec((tm, D), lambda *g: (tile_of(*g), 0)),
        pl.BlockSpec((HALO, D), lambda *g: (jnp.minimum((tile_of(*g) + 1) * r, last), 0)),
    ]


def _halo_keep(t0, tm, L):
    return t0 > 0, t0 + tm < L


def _fill_hcat(hcat_ref, hp_ref, hm_ref, hn_ref, tm, t0, L):
    keep_prev, keep_next = _halo_keep(t0, tm, L)
    hcat_ref[pl.ds(0, HALO), :] = jnp.where(keep_prev, hp_ref[...], jnp.zeros_like(hp_ref))
    hcat_ref[pl.ds(HALO, tm), :] = hm_ref[...]
    hcat_ref[pl.ds(HALO + tm, HALO), :] = jnp.where(keep_next, hn_ref[...], jnp.zeros_like(hn_ref))


def _pool_chunks(tm):
    rows = tm + 2 * HALO
    ch = min(tm, 256)
    kw = min(rows, 512)
    return ch, kw, [(c * ch, min(c * ch, rows - kw)) for c in range(tm // ch)]


def _pool_kernel(xp_ref, xm_ref, xn_ref, g_ref, w_ref, pw_ref, ps_ref, o_ref, h_ref, hcat_ref, band_ref,
                 *, tm, L):
    rows = tm + 2 * HALO
    gw = pw_ref.shape[-1]
    ch, kw, chunks = _pool_chunks(tm)

    @pl.when(pl.program_id(0) == 0)
    def _():
        r = lax.broadcasted_iota(jnp.int32, (ch, kw), 0)
        s = lax.broadcasted_iota(jnp.int32, (ch, kw), 1)
        for g, w in enumerate(POOL_WINDOWS):
            for c, (r0, s0) in enumerate(chunks):
                d = (s + (s0 - HALO)) - (r + r0)
                band_ref[g, c] = jnp.where((d >= -(w // 2)) & (d < w - w // 2), 1.0, 0.0).astype(MXU_DTYPE)

    t0 = (pl.program_id(0) * tm) % L
    keep_prev, keep_next = _halo_keep(t0, tm, L)
    for x_ref, row0, n, keep in ((xp_ref, 0, HALO, keep_prev), (xm_ref, HALO, tm, True),
                                 (xn_ref, HALO + tm, HALO, keep_next)):
        hx = (_rms(x_ref[...]) * g_ref[...]).astype(hcat_ref.dtype)
        hcat_ref[pl.ds(row0, n), :] = hx if keep is True else jnp.where(keep, hx, jnp.zeros_like(hx))
    h_ref[...] = hcat_ref[pl.ds(HALO, tm), :]
    p = _dot(hcat_ref[...], w_ref[...])
    p_hi = p.astype(MXU_DTYPE)
    p_lo = (p - p_hi.astype(F32)).astype(MXU_DTYPE)
    tc = t0 + lax.broadcasted_iota(jnp.int32, (tm, 1), 0)
    for g, w in enumerate(POOL_WINDOWS):
        lo_off, hi_off = w // 2, w - w // 2
        sl = slice(g * gw, (g + 1) * gw)
        wsum = jnp.concatenate(
            [_dot(band_ref[g, c], p_hi[s0:s0 + kw, sl]) + _dot(band_ref[g, c], p_lo[s0:s0 + kw, sl])
             for c, (_, s0) in enumerate(chunks)], axis=0)
        cnt = (jnp.minimum(tc + hi_off, L) - jnp.maximum(tc - lo_off, 0)).astype(F32)
        pooled = wsum / cnt - p[HALO:HALO + tm, sl]
        mixed = _dot(pooled.astype(MXU_DTYPE), pw_ref[g]) * ps_ref[:, sl]
        o_ref[:, sl] = mixed.astype(o_ref.dtype)


def _pool_call(x2, g, w_pool, pool_w, pool_scale, L):
    T, D = x2.shape
    Dp = w_pool.shape[1]
    G, gw, _ = pool_w.shape
    tm = _pick(L, 512, HALO)
    ch, kw, chunks = _pool_chunks(tm)
    return pl.pallas_call(
        functools.partial(_pool_kernel, tm=tm, L=L),
        grid=(T // tm,),
        in_specs=_halo_specs(tm, T, D, lambda i: i) + [
            pl.BlockSpec((1, D), lambda i: (0, 0)),
            pl.BlockSpec((D, Dp), lambda i: (0, 0)),
            pl.BlockSpec((G, gw, gw), lambda i: (0, 0, 0)),
            pl.BlockSpec((1, Dp), lambda i: (0, 0)),
        ],
        out_specs=[pl.BlockSpec((tm, Dp), lambda i: (i, 0)),
                   pl.BlockSpec((tm, D), lambda i: (i, 0))],
        out_shape=[jax.ShapeDtypeStruct((T, Dp), MXU_DTYPE),
                   jax.ShapeDtypeStruct((T, D), MXU_DTYPE)],
        scratch_shapes=[pltpu.VMEM((tm + 2 * HALO, D), MXU_DTYPE),
                        pltpu.VMEM((len(POOL_WINDOWS), len(chunks), ch, kw), MXU_DTYPE)],
        compiler_params=_params("arbitrary"),
        name="pool",
    )(x2, x2, x2, g.reshape(1, D), w_pool, pool_w, pool_scale.reshape(1, Dp))


def _hyproj_kernel(hp_ref, hm_ref, hn_ref, w_ref, cw_ref, cb_ref, o_ref, hcat_ref, *, tm, L):
    rows = tm + 2 * HALO
    t0 = (pl.program_id(0) * tm) % L

    @pl.when(pl.program_id(1) == 0)
    def _():
        _fill_hcat(hcat_ref, hp_ref, hm_ref, hn_ref, tm, t0, L)

    p = _dot(hcat_ref[...], w_ref[...])
    prev = pltpu.roll(p, 1, 0)
    nxt = pltpu.roll(p, rows - 1, 0)
    y = prev * cw_ref[0:1, :] + p * cw_ref[1:2, :] + nxt * cw_ref[2:3, :] + cb_ref[...]
    o_ref[...] = y[HALO:HALO + tm]


def _hyproj_call(h, w_hy, conv_w, conv_b, L, C):
    T, D = h.shape
    n3 = w_hy.shape[1]
    tn = C if C % LANES == 0 else n3
    tm = _pick(L, 512, HALO)
    return pl.pallas_call(
        functools.partial(_hyproj_kernel, tm=tm, L=L),
        grid=(T // tm, n3 // tn),
        in_specs=_halo_specs(tm, T, D, lambda i, j: i) + [
            pl.BlockSpec((D, tn), lambda i, j: (0, j)),
            pl.BlockSpec((conv_w.shape[0], tn), lambda i, j: (0, j)),
            pl.BlockSpec((1, tn), lambda i, j: (0, j)),
        ],
        out_specs=pl.BlockSpec((tm, tn), lambda i, j: (i, j)),
        out_shape=jax.ShapeDtypeStruct((T, n3), F32),
        scratch_shapes=[pltpu.VMEM((tm + 2 * HALO, D), MXU_DTYPE)],
        compiler_params=_params("parallel", "arbitrary"),
        name="hyproj",
    )(h, h, h, w_hy, conv_w, conv_b.reshape(1, n3))


def _filter_kernel(w1_ref, b1_ref, f1_ref, w2_ref, b2_ref, f2_ref, w3_ref, two_ref, ss_ref,
                   *, L, C, fb, tmf):
    i = pl.program_id(0)
    N = 2 * L
    hp = lax.Precision.HIGHEST
    m = i * tmf + lax.broadcasted_iota(jnp.int32, (tmf, 1), 0)
    tf = jnp.where(m < L, m, N - m).astype(F32)
    tt = tf / (L - 1)
    wt = (2.0 * math.pi) * tf / L
    lane = lax.broadcasted_iota(jnp.int32, (1, LANES), 1)
    bidx = jnp.where(lane <= fb, lane - 1, lane - 1 - fb).astype(F32)
    bands = 1e-4 + bidx * ((fb - 1 - 1e-4) / (fb - 1))
    ang = bands * wt + jnp.where(lane > fb, 0.5 * math.pi, 0.0)
    z = jnp.where(lane == 0, tt, jnp.where(lane <= 2 * fb, jnp.cos(ang), 0.0))
    h = jnp.sin(f1_ref[...] * (_dot(z, w1_ref[...], precision=hp) + b1_ref[...]))
    h = jnp.sin(f2_ref[...] * (_dot(h, w2_ref[...], precision=hp) + b2_ref[...]))
    h_hi = h.astype(MXU_DTYPE)
    h_lo = (h - h_hi.astype(F32)).astype(MXU_DTYPE)
    hk = jnp.concatenate([h_hi, h_hi, h_lo], axis=1)
    max_decay = math.log(DECAY_TARGET) / FAST_DECAY_PCT
    min_decay = math.log(DECAY_TARGET) / SLOW_DECAY_PCT
    cl = lax.broadcasted_iota(jnp.int32, (1, C), 1).astype(F32)
    deltas = jnp.abs(min_decay + cl * ((max_decay - min_decay) / (C - 1)))
    decay = jnp.where(m == L, 0.0, jnp.exp(-tt * deltas))
    decay2 = jnp.concatenate([decay, decay], axis=1)

    def emit(two):
        two_ref[0] = two[:, :C]
        two_ref[1] = two[:, C:]
        ss_ref[...] += jnp.sum(two * two, axis=0, keepdims=True)

    @pl.when(i == 0)
    def _():
        ss_ref[...] = jnp.zeros_like(ss_ref)
        lag0 = jnp.where(m == 0, 1.0, 0.0)
        emit(decay2 * (_dot(hk, w3_ref[0]) + lag0 * _dot(hk, w3_ref[1])))

    @pl.when(i != 0)
    def _():
        emit(decay2 * _dot(hk, w3_ref[(i * tmf >= L).astype(jnp.int32)]))


def _filter_call(L, C, w1, b1, f1, w2, b2, f2, w3):
    N = 2 * L
    fe, fh = w1.shape
    fb = (fe - 1) // 2
    assert fe <= LANES
    w1p = jnp.zeros((LANES, fh), F32).at[:fe].set(w1)
    tmf = _pick(L, 512)
    w3d = jnp.transpose(w3.reshape(fh, 2, 2, C), (2, 0, 1, 3)).reshape(2, fh, 2 * C)
    w3_hi = w3d.astype(MXU_DTYPE)
    w3_lo = (w3d - w3_hi.astype(F32)).astype(MXU_DTYPE)
    w3k = jnp.concatenate([w3_hi, w3_lo, w3_hi], axis=1)
    full = lambda a: pl.BlockSpec(a.shape, lambda i: (0,) * a.ndim)
    args = (w1p, b1.reshape(1, fh), f1.reshape(1, fh), w2, b2.reshape(1, fh), f2.reshape(1, fh), w3k)
    two, ss = pl.pallas_call(
        functools.partial(_filter_kernel, L=L, C=C, fb=fb, tmf=tmf),
        grid=(N // tmf,),
        in_specs=[full(a) for a in args],
        out_specs=[pl.BlockSpec((2, tmf, C), lambda i: (0, i, 0)),
                   pl.BlockSpec((1, 2 * C), lambda i: (0, 0))],
        out_shape=[jax.ShapeDtypeStruct((2, N, C), F32),
                   jax.ShapeDtypeStruct((1, 2 * C), F32)],
        compiler_params=_params("arbitrary"),
        name="filter",
    )(*args)
    return two, ss.reshape(2, 1, C)


def _stack(mr, mi):
    return jnp.concatenate([jnp.concatenate([mr, -mi], axis=-1),
                            jnp.concatenate([mi, mr], axis=-1)], axis=-2)


def _il_rows(m):
    n = m.shape[-2] // 2
    return jnp.stack([m[..., :n, :], m[..., n:, :]], axis=-2).reshape(m.shape)


def _il_cols(m):
    return jnp.swapaxes(_il_rows(jnp.swapaxes(m, -1, -2)), -1, -2)


def _fft_tables(nc):
    N = nc * NF
    na = nc // 2
    i32 = jnp.int32
    k1 = jnp.arange(nc, dtype=i32)
    b = jnp.arange(NF, dtype=i32)
    ang = (-2.0 * math.pi / nc) * ((k1[:, None] * k1[None, :]) % nc).astype(F32)
    fr, fi = jnp.cos(ang)[None], jnp.sin(ang)[None]
    ang = (-2.0 * math.pi / N) * (b[:, None] * k1[None, :]).astype(F32)
    tr, ti = jnp.cos(ang)[:, :, None], jnp.sin(ang)[:, :, None]
    ar, ai = tr * fr - ti * fi, tr * fi + ti * fr
    a_data = _il_rows(_stack(ar[:, :, :na], ai[:, :, :na]))
    a_filt = _il_rows(jnp.concatenate([ar, ai], axis=1))
    cr, ci = jnp.swapaxes(ar, 1, 2)[:, :na] / nc, -jnp.swapaxes(ai, 1, 2)[:, :na] / nc
    c_inv = _il_cols(_stack(cr, ci))
    ang = (-2.0 * math.pi / NF) * ((b[:, None] * b[None, :]) % NF).astype(F32)
    gr, gi = jnp.cos(ang), jnp.sin(ang)
    g_fwd = _il_cols(_stack(gr, gi))
    g_inv = _il_rows(_stack(gr.T / NF, -gi.T / NF))
    cast = lambda t: t.astype(MXU_DTYPE)
    return cast(a_data), cast(a_filt), cast(g_fwd), cast(g_inv), cast(c_inv)


NFP = NF + 8
U32 = jnp.uint32
PACKED = True


def _fine_per_step(nc):
    return max(8, min(32, (16 * 128) // nc))


def _rows2d(ref):
    return ref.reshape(math.prod(ref.shape[:-1]), ref.shape[-1])


def _ld_rows(ref, start, n, stride):
    return _rows2d(ref)[pl.ds(start, n, stride=stride), :]


def _st_rows(ref, start, n, stride, val):
    _rows2d(ref)[pl.ds(start, n, stride=stride), :] = val


def _spec_scratch(nc):
    if PACKED:
        return pltpu.VMEM((2, 1, nc * NFP, LANES), U32)
    return pltpu.VMEM((2, 2, nc * NFP, LANES), F32)


def _spec_store(y_ref, rows, v):
    if PACKED:
        w = pltpu.bitcast(v.astype(jnp.bfloat16), U32)
        y_ref[0, 0, rows, :] = w[:, :LANES]
        y_ref[1, 0, rows, :] = w[:, LANES:]
    else:
        for half in range(2):
            lanes = slice(half * LANES, (half + 1) * LANES)
            y_ref[half, 0, rows, :] = v[0::2, lanes]
            y_ref[half, 1, rows, :] = v[1::2, lanes]


def _spec_load(y_ref, rows):
    if PACKED:
        w = jnp.concatenate([y_ref[0, 0, rows, :], y_ref[1, 0, rows, :]], axis=1)
        return pltpu.bitcast(w, jnp.bfloat16)
    re = jnp.concatenate([y_ref[0, 0, rows, :], y_ref[1, 0, rows, :]], axis=1)
    im = jnp.concatenate([y_ref[0, 1, rows, :], y_ref[1, 1, rows, :]], axis=1)
    return jnp.stack([re, im], axis=1).reshape(2 * re.shape[0], re.shape[1]).astype(MXU_DTYPE)


def _phase_a(z_refs, at_ref, y_ref, step, *, planes, na, nc):
    bb = at_ref.shape[0]
    for b in range(bb):
        x = jnp.concatenate(
            [jnp.concatenate([_ld_rows(z, p * na * bb + b, na, bb) for p in range(planes)], axis=0)
             for z in z_refs], axis=1)
        r = _dot(at_ref[b], x.astype(MXU_DTYPE))
        _spec_store(y_ref, pl.ds(step * bb + b, nc, stride=NFP), r)


def _k1_rows(step, kk, k):
    return pl.ds(pl.multiple_of((step * kk + k) * NFP, 8), NF)


def _lconv_kernel(z0_ref, z1_ref, gate_ref, at_ref, ct_ref, gf_ref, gi_ref, hr_ref, hi_ref,
                  o_ref, y_ref, yc_ref, *, na, nc, kk, sa, sb):
    s = pl.program_id(2)

    @pl.when(s < sa)
    def _():
        _phase_a((z0_ref, z1_ref), at_ref, y_ref, s, planes=2, na=na, nc=nc)

    @pl.when((s >= sa) & (s < sa + sb))
    def _():
        for k in range(kk):
            rows = _k1_rows(s - sa, kk, k)
            z = _dot(gf_ref[...], _spec_load(y_ref, rows))
            zr, zi = z[:NF], z[NF:]
            hr, hi = hr_ref[k].astype(F32), hi_ref[k].astype(F32)
            x2 = jnp.concatenate([zr * hr - zi * hi, zr * hi + zi * hr], axis=0).astype(MXU_DTYPE)
            u = _dot(gi_ref[...], x2)
            _spec_store(y_ref, rows, u)

    @pl.when(s >= sa + sb)
    def _():
        step = s - sa - sb
        bb, ycp = ct_ref.shape[0], yc_ref.shape[2]
        for b in range(bb):
            x = _spec_load(y_ref, pl.ds(step * bb + b, nc, stride=NFP))
            r = _dot(ct_ref[b], x)
            for half in range(2):
                _st_rows(yc_ref.at[half], b, 2 * na, ycp, r[:, half * LANES:(half + 1) * LANES])
        for half in range(2):
            lanes = slice(half * LANES, (half + 1) * LANES)
            y = yc_ref[half, :, :bb, :].reshape(2, na, bb, LANES)
            o_ref[:, :, :, lanes] = (gate_ref[:, :, :, lanes] * y).astype(o_ref.dtype)


def _lconv_call(z5, z_off, gate5, gate_off, spec, order, tabs, na, out_dtype):
    a_data, _, g_fwd, g_inv, c_inv = tabs
    P = z5.shape[0]
    C = spec[0].shape[-1]
    nc = 2 * na
    ct = 2 * LANES
    kk = _pick(nc, 16, 1)
    BB = _fine_per_step(nc)
    YCP = BB + 8
    sa, sb = NF // BB, nc // kk
    zo, go = z_off // LANES, gate_off // ct
    ia = lambda s: jnp.minimum(s, sa - 1)
    ib = lambda s: jnp.clip(s - sa, 0, sb - 1)
    ic = lambda s: jnp.maximum(s - sa - sb, 0)
    zspec = lambda h: pl.BlockSpec((None, 2, na, BB, LANES), lambda p, jc, s: (p, 0, 0, ia(s), zo + 2 * jc + h))
    hspec = pl.BlockSpec((None, kk, NF, ct), lambda p, jc, s: (order, ib(s), 0, jc))
    gspec = pl.BlockSpec((2 * NF, 2 * NF), lambda p, jc, s: (0, 0))
    return pl.pallas_call(
        functools.partial(_lconv_kernel, na=na, nc=nc, kk=kk, sa=sa, sb=sb),
        grid=(P, C // ct, 2 * sa + sb),
        in_specs=[zspec(0), zspec(1),
                  pl.BlockSpec((None, 2, na, BB, ct), lambda p, jc, s: (p, 0, 0, ic(s), go + jc)),
                  pl.BlockSpec((BB, 2 * nc, nc), lambda p, jc, s: (ia(s), 0, 0)),
                  pl.BlockSpec((BB, nc, 2 * nc), lambda p, jc, s: (ic(s), 0, 0)),
                  gspec, gspec, hspec, hspec],
        out_specs=pl.BlockSpec((None, 2, na, BB, ct), lambda p, jc, s: (p, 0, 0, ic(s), jc)),
        out_shape=jax.ShapeDtypeStruct((P, 2, na, NF, C), out_dtype),
        scratch_shapes=[_spec_scratch(nc), pltpu.VMEM((2, 2 * na, YCP, LANES), F32)],
        compiler_params=_params("parallel", "parallel", "arbitrary"),
        name="lconv",
    )(z5, z5, gate5, a_data, c_inv, g_fwd, g_inv, spec[0], spec[1])


def _fspec_kernel(t0_ref, t1_ref, at_ref, gf_ref, ss_ref, bias_ref, hr_ref, hi_ref, y_ref,
                  *, nc, kk, sa):
    s = pl.program_id(2)

    @pl.when(s < sa)
    def _():
        _phase_a((t0_ref, t1_ref), at_ref, y_ref, s, planes=1, na=nc, nc=nc)

    @pl.when(s >= sa)
    def _():
        scale = lax.rsqrt(ss_ref[...] + EPS)
        for k in range(kk):
            z = _dot(gf_ref[...], _spec_load(y_ref, _k1_rows(s - sa, kk, k)))
            hr_ref[k] = (z[:NF] * scale + bias_ref[...]).astype(hr_ref.dtype)
            hi_ref[k] = (z[NF:] * scale).astype(hi_ref.dtype)


def _fspec_call(two, ss, bias, tabs, nc):
    _, a_filt, g_fwd, _, _ = tabs
    C = two.shape[-1]
    ct = 2 * LANES
    kk = _pick(nc, 16, 1)
    BB = _fine_per_step(nc)
    sa, sb = NF // BB, nc // kk
    two5 = two.reshape(2, 1, nc, NF, C)
    ia = lambda s: jnp.minimum(s, sa - 1)
    ib = lambda s: jnp.maximum(s - sa, 0)
    tspec = lambda h: pl.BlockSpec((None, 1, nc, BB, LANES), lambda o, jc, s: (o, 0, 0, ia(s), 2 * jc + h))
    vspec = pl.BlockSpec((None, 1, ct), lambda o, jc, s: (o, 0, jc))
    hspec = pl.BlockSpec((None, kk, NF, ct), lambda o, jc, s: (o, ib(s), 0, jc))
    out = jax.ShapeDtypeStruct((2, nc, NF, C), MXU_DTYPE)
    return pl.pallas_call(
        functools.partial(_fspec_kernel, nc=nc, kk=kk, sa=sa),
        grid=(2, C // ct, sa + sb),
        in_specs=[tspec(0), tspec(1),
                  pl.BlockSpec((BB, 2 * nc, nc), lambda o, jc, s: (ia(s), 0, 0)),
                  pl.BlockSpec((2 * NF, 2 * NF), lambda o, jc, s: (0, 0)),
                  vspec, vspec],
        out_specs=[hspec, hspec],
        out_shape=[out, out],
        scratch_shapes=[_spec_scratch(nc)],
        compiler_params=_params("parallel", "parallel", "arbitrary"),
        name="fspec",
    )(two5, two5, a_filt, g_fwd, ss, bias)


def _sigmoid(x):
    return 0.5 * (jnp.tanh(0.5 * x) + 1.0)


def _merge_kernel(h_ref, a_ref, b_ref, x_ref, wga_ref, wgb_ref, wa_ref, wb_ref, wo_ref, o_ref,
                  m0_ref, m1_ref, acc_ref, *, nj):
    q = pl.program_id(0)
    first = (jnp.maximum(q - 1, 0) % nj) == 0

    @pl.when(q == 0)
    def _():
        m1_ref[...] = jnp.zeros_like(m1_ref)
        acc_ref[...] = jnp.zeros_like(acc_ref)

    def body(m_new_ref, m_old_ref):
        contrib = _dot(m_old_ref[...], wo_ref[...])
        acc = jnp.where(first, x_ref[...], acc_ref[...]) + contrib
        acc_ref[...] = acc
        o_ref[...] = acc
        h = h_ref[...]
        ga = _sigmoid(_dot(h, wga_ref[...]))
        gb = _sigmoid(_dot(h, wgb_ref[...]))
        m = ga * _dot(a_ref[...], wa_ref[...]) + gb * _dot(b_ref[...], wb_ref[...])
        m_new_ref[...] = m.astype(m_new_ref.dtype)

    @pl.when(q % 2 == 0)
    def _():
        body(m0_ref, m1_ref)

    @pl.when(q % 2 == 1)
    def _():
        body(m1_ref, m0_ref)


def _merge_call(h, a, b, x2, w_ga, w_gb, wa, wb, wo):
    T, D = x2.shape
    Dp, C = a.shape[1], b.shape[1]
    tm = _pick(T, 512)
    tn = _pick(D, 512, LANES)
    nj = D // tn
    Q = (T // tm) * nj
    cur = lambda q: jnp.minimum(q, Q - 1)
    pend = lambda q: jnp.maximum(q - 1, 0)
    row = lambda f: (lambda q: (f(q) // nj, 0))
    col = lambda q: (0, cur(q) % nj)
    return pl.pallas_call(
        functools.partial(_merge_kernel, nj=nj),
        grid=(Q + 1,),
        in_specs=[pl.BlockSpec((tm, D), row(cur)),
                  pl.BlockSpec((tm, Dp), row(cur)),
                  pl.BlockSpec((tm, C), row(cur)),
                  pl.BlockSpec((tm, D), row(pend)),
                  pl.BlockSpec((D, tn), col),
                  pl.BlockSpec((D, tn), col),
                  pl.BlockSpec((Dp, tn), col),
                  pl.BlockSpec((C, tn), col),
                  pl.BlockSpec((tn, D), lambda q: (pend(q) % nj, 0))],
        out_specs=pl.BlockSpec((tm, D), row(pend)),
        out_shape=jax.ShapeDtypeStruct((T, D), F32),
        scratch_shapes=[pltpu.VMEM((tm, tn), MXU_DTYPE), pltpu.VMEM((tm, tn), MXU_DTYPE),
                        pltpu.VMEM((tm, D), F32)],
        compiler_params=_params("arbitrary"),
        name="merge",
    )(h, a, b, x2, w_ga, w_gb, wa, wb, wo)


def _ffn_kernel(x_ref, g_ref, wg_ref, wu_ref, wd_ref, gf_ref, o_ref, h_ref):
    j = pl.program_id(1)

    @pl.when(j == 0)
    def _():
        x = x_ref[...]
        h_ref[...] = (_rms(x) * g_ref[...]).astype(h_ref.dtype)
        o_ref[...] = x

    h = h_ref[...]
    act = jax.nn.silu(_dot(h, wg_ref[...])) * _dot(h, wu_ref[...])
    o_ref[...] += _dot(act.astype(MXU_DTYPE), wd_ref[...])

    @pl.when(j == pl.num_programs(1) - 1)
    def _():
        o_ref[...] = _rms(o_ref[...]) * gf_ref[...]


def _ffn_call(x2, g, wg, wu, wd, gf):
    T, D = x2.shape
    Dff = wg.shape[1]
    tm = _pick(T, 1024)
    tf = _pick(Dff, 512, LANES)
    return pl.pallas_call(
        _ffn_kernel,
        grid=(T // tm, Dff // tf),
        in_specs=[pl.BlockSpec((tm, D), lambda i, j: (i, 0)),
                  pl.BlockSpec((1, D), lambda i, j: (0, 0)),
                  pl.BlockSpec((D, tf), lambda i, j: (0, j)),
                  pl.BlockSpec((D, tf), lambda i, j: (0, j)),
                  pl.BlockSpec((tf, D), lambda i, j: (j, 0)),
                  pl.BlockSpec((1, D), lambda i, j: (0, 0))],
        out_specs=pl.BlockSpec((tm, D), lambda i, j: (i, 0)),
        out_shape=jax.ShapeDtypeStruct((T, D), F32),
        scratch_shapes=[pltpu.VMEM((tm, D), MXU_DTYPE)],
        compiler_params=_params("parallel", "arbitrary"),
        name="ffn",
    )(x2, g.reshape(1, D), wg, wu, wd, gf.reshape(1, D))


def _trunk(x, w):
    B, L, D = x.shape
    T = B * L
    C = w["hyena_bias"].shape[1]
    assert B % 2 == 0 and L % NF == 0 and C % (2 * LANES) == 0 and w["hyena_bias"].shape[0] == 2
    na = L // NF
    nc = 2 * na
    P = B // 2
    x2 = x.reshape(T, D)

    a, h = _pool_call(x2, w["g_mix"], w["w_pool"], w["pool_w"], w["pool_scale"], L)
    uc = _hyproj_call(h, w["w_hy"], w["conv_w"], w["conv_b"], L, C)
    uc5 = uc.reshape(P, 2, na, NF, 3 * C)

    tabs = _fft_tables(nc)
    two, ss = _filter_call(L, C, *w["filt"])
    spec = _fspec_call(two, ss, w["hyena_bias"].reshape(2, 1, C), tabs, nc)

    z1 = _lconv_call(uc5, 0, uc5, C, spec, 0, tabs, na, F32)
    b = _lconv_call(z1, 0, uc5, 2 * C, spec, 1, tabs, na, MXU_DTYPE).reshape(T, C)

    xn = _merge_call(h, a, b, x2, w["w_ga"], w["w_gb"], w["wa"], w["wb"], w["wo"])
    y = _ffn_call(xn, w["g_ffn"], w["wg"], w["wu"], w["wd"], w["g_final"])
    return y.reshape(B, L, D)


def kernel(x_prompt, x_sample, g_mix, w_in, pool_w, pool_scale, conv_w, conv_b, filt_w1, filt_b1, filt_freq1, filt_w2, filt_b2, filt_freq2, filt_w3, hyena_bias, w_branch_a, w_branch_b, w_out, g_ffn, w_gate, w_up, w_down, g_final):
    assert g_mix.shape[0] == 1, "depth-1 block"
    D = x_prompt.shape[-1]
    Dp = pool_scale.shape[1]
    C = hyena_bias.shape[2]
    s1, s2, s3 = Dp, Dp + 3 * C, Dp + 3 * C + D
    cast = lambda t: t.astype(MXU_DTYPE)
    w = {
        "g_mix": g_mix[0], "w_pool": cast(w_in[0][:, :s1]), "w_hy": cast(w_in[0][:, s1:s2]),
        "w_ga": cast(w_in[0][:, s2:s3]), "w_gb": cast(w_in[0][:, s3:]),
        "pool_w": cast(pool_w[0]), "pool_scale": pool_scale[0],
        "conv_w": conv_w[0], "conv_b": conv_b[0],
        "filt": (filt_w1[0], filt_b1[0], filt_freq1[0], filt_w2[0], filt_b2[0], filt_freq2[0], filt_w3[0]),
        "hyena_bias": hyena_bias[0],
        "wa": cast(w_branch_a[0]), "wb": cast(w_branch_b[0]), "wo": cast(w_out[0]),
        "g_ffn": g_ffn[0], "wg": cast(w_gate[0]), "wu": cast(w_up[0]), "wd": cast(w_down[0]),
        "g_final": g_final,
    }
    return (_trunk(x_prompt, w), _trunk(x_sample, w))
```

```python
import functools
import math

import jax
import jax.numpy as jnp
from jax import lax
from jax.experimental import pallas as pl
from jax.experimental.pallas import tpu as pltpu

F32 = jnp.float32
MXU_DTYPE = jnp.bfloat16

EPS = 1e-6
POOL_WINDOWS = (2, 4, 8, 16)
DECAY_TARGET = 1e-2
FAST_DECAY_PCT = 0.3
SLOW_DECAY_PCT = 1.5

LANES = 128
NF = 128
HALO = 16
VMEM_LIMIT = 56 * 1024 * 1024


def _pick(n, pref, mult=8):
    best = None
    for d in range(mult, min(n, pref) + 1, mult):
        if n % d == 0:
            best = d
    return best if best is not None else n


def _params(*sem):
    return pltpu.CompilerParams(dimension_semantics=sem, vmem_limit_bytes=VMEM_LIMIT)


def _dot(a, b, **kw):
    return jnp.dot(a, b, preferred_element_type=F32, **kw)


def _rms(x):
    return x * lax.rsqrt(jnp.mean(x * x, axis=-1, keepdims=True) + EPS)


def _halo_specs(tm, T, D, tile_of):
    r = tm // HALO
    last = T // HALO - 1
    return [
        pl.BlockSpec((HALO, D), lambda *g: (jnp.maximum(tile_of(*g) * r - 1, 0), 0)),
        pl.BlockSpec((tm, D), lambda *g: (tile_of(*g), 0)),
        pl.BlockSpec((HALO, D), lambda *g: (jnp.minimum((tile_of(*g) + 1) * r, last), 0)),
    ]


def _halo_keep(t0, tm, L):
    return t0 > 0, t0 + tm < L


def _fill_hcat(hcat_ref, hp_ref, hm_ref, hn_ref, tm, t0, L):
    keep_prev, keep_next = _halo_keep(t0, tm, L)
    hcat_ref[pl.ds(0, HALO), :] = jnp.where(keep_prev, hp_ref[...], jnp.zeros_like(hp_ref))
    hcat_ref[pl.ds(HALO, tm), :] = hm_ref[...]
    hcat_ref[pl.ds(HALO + tm, HALO), :] = jnp.where(keep_next, hn_ref[...], jnp.zeros_like(hn_ref))


def _pool_chunks(tm):
    rows = tm + 2 * HALO
    ch = min(tm, 256)
    kw = min(rows, 512)
    return ch, kw, [(c * ch, min(c * ch, rows - kw)) for c in range(tm // ch)]


def _pool_kernel(xp_ref, xm_ref, xn_ref, g_ref, w_ref, pw_ref, ps_ref, o_ref, h_ref, hcat_ref, band_ref,
                 *, tm, L):
    rows = tm + 2 * HALO
    gw = pw_ref.shape[-1]
    ch, kw, chunks = _pool_chunks(tm)

    @pl.when(pl.program_id(0) == 0)
    def _():
        r = lax.broadcasted_iota(jnp.int32, (ch, kw), 0)
        s = lax.broadcasted_iota(jnp.int32, (ch, kw), 1)
        for g, w in enumerate(POOL_WINDOWS):
            for c, (r0, s0) in enumerate(chunks):
                d = (s + (s0 - HALO)) - (r + r0)
                band_ref[g, c] = jnp.where((d >= -(w // 2)) & (d < w - w // 2), 1.0, 0.0).astype(MXU_DTYPE)

    t0 = (pl.program_id(0) * tm) % L
    keep_prev, keep_next = _halo_keep(t0, tm, L)
    for x_ref, row0, n, keep in ((xp_ref, 0, HALO, keep_prev), (xm_ref, HALO, tm, True),
                                 (xn_ref, HALO + tm, HALO, keep_next)):
        hx = (_rms(x_ref[...]) * g_ref[...]).astype(hcat_ref.dtype)
        hcat_ref[pl.ds(row0, n), :] = hx if keep is True else jnp.where(keep, hx, jnp.zeros_like(hx))
    h_ref[...] = hcat_ref[pl.ds(HALO, tm), :]
    p = _dot(hcat_ref[...], w_ref[...])
    p_hi = p.astype(MXU_DTYPE)
    p_lo = (p - p_hi.astype(F32)).astype(MXU_DTYPE)
    tc = t0 + lax.broadcasted_iota(jnp.int32, (tm, 1), 0)
    for g, w in enumerate(POOL_WINDOWS):
        lo_off, hi_off = w // 2, w - w // 2
        sl = slice(g * gw, (g + 1) * gw)
        wsum = jnp.concatenate(
            [_dot(band_ref[g, c], p_hi[s0:s0 + kw, sl]) + _dot(band_ref[g, c], p_lo[s0:s0 + kw, sl])
             for c, (_, s0) in enumerate(chunks)], axis=0)
        cnt = (jnp.minimum(tc + hi_off, L) - jnp.maximum(tc - lo_off, 0)).astype(F32)
        pooled = wsum / cnt - p[HALO:HALO + tm, sl]
        mixed = _dot(pooled.astype(MXU_DTYPE), pw_ref[g]) * ps_ref[:, sl]
        o_ref[:, sl] = mixed.astype(o_ref.dtype)


def _pool_call(x2, g, w_pool, pool_w, pool_scale, L):
    T, D = x2.shape
    Dp = w_pool.shape[1]
    G, gw, _ = pool_w.shape
    tm = _pick(L, 1024, HALO)
    ch, kw, chunks = _pool_chunks(tm)
    return pl.pallas_call(
        functools.partial(_pool_kernel, tm=tm, L=L),
        grid=(T // tm,),
        in_specs=_halo_specs(tm, T, D, lambda i: i) + [
            pl.BlockSpec((1, D), lambda i: (0, 0)),
            pl.BlockSpec((D, Dp), lambda i: (0, 0)),
            pl.BlockSpec((G, gw, gw), lambda i: (0, 0, 0)),
            pl.BlockSpec((1, Dp), lambda i: (0, 0)),
        ],
        out_specs=[pl.BlockSpec((tm, Dp), lambda i: (i, 0)),
                   pl.BlockSpec((tm, D), lambda i: (i, 0))],
        out_shape=[jax.ShapeDtypeStruct((T, Dp), MXU_DTYPE),
                   jax.ShapeDtypeStruct((T, D), MXU_DTYPE)],
        scratch_shapes=[pltpu.VMEM((tm + 2 * HALO, D), MXU_DTYPE),
                        pltpu.VMEM((len(POOL_WINDOWS), len(chunks), ch, kw), MXU_DTYPE)],
        compiler_params=_params("arbitrary"),
        name="pool",
    )(x2, x2, x2, g.reshape(1, D), w_pool, pool_w, pool_scale.reshape(1, Dp))


def _hyproj_kernel(hp_ref, hm_ref, hn_ref, w_ref, cw_ref, cb_ref, o_ref, hcat_ref, *, tm, L):
    rows = tm + 2 * HALO
    t0 = (pl.program_id(0) * tm) % L

    @pl.when(pl.program_id(1) == 0)
    def _():
        _fill_hcat(hcat_ref, hp_ref, hm_ref, hn_ref, tm, t0, L)

    p = _dot(hcat_ref[...], w_ref[...])
    prev = pltpu.roll(p, 1, 0)
    nxt = pltpu.roll(p, rows - 1, 0)
    y = prev * cw_ref[0:1, :] + p * cw_ref[1:2, :] + nxt * cw_ref[2:3, :] + cb_ref[...]
    o_ref[...] = y[HALO:HALO + tm]


def _hyproj_call(h, w_hy, conv_w, conv_b, L, C):
    T, D = h.shape
    n3 = w_hy.shape[1]
    tn = C if C % LANES == 0 else n3
    tm = _pick(L, 1024, HALO)
    return pl.pallas_call(
        functools.partial(_hyproj_kernel, tm=tm, L=L),
        grid=(T // tm, n3 // tn),
        in_specs=_halo_specs(tm, T, D, lambda i, j: i) + [
            pl.BlockSpec((D, tn), lambda i, j: (0, j)),
            pl.BlockSpec((conv_w.shape[0], tn), lambda i, j: (0, j)),
            pl.BlockSpec((1, tn), lambda i, j: (0, j)),
        ],
        out_specs=pl.BlockSpec((tm, tn), lambda i, j: (i, j)),
        out_shape=jax.ShapeDtypeStruct((T, n3), F32),
        scratch_shapes=[pltpu.VMEM((tm + 2 * HALO, D), MXU_DTYPE)],
        compiler_params=_params("parallel", "arbitrary"),
        name="hyproj",
    )(h, h, h, w_hy, conv_w, conv_b.reshape(1, n3))


def _filter_kernel(w1_ref, b1_ref, f1_ref, w2_ref, b2_ref, f2_ref, w3_ref, two_ref, ss_ref,
                   *, L, C, fb, tmf):
    i = pl.program_id(0)
    N = 2 * L
    hp = lax.Precision.HIGHEST
    m = i * tmf + lax.broadcasted_iota(jnp.int32, (tmf, 1), 0)
    tf = jnp.where(m < L, m, N - m).astype(F32)
    tt = tf / (L - 1)
    wt = (2.0 * math.pi) * tf / L
    lane = lax.broadcasted_iota(jnp.int32, (1, LANES), 1)
    bidx = jnp.where(lane <= fb, lane - 1, lane - 1 - fb).astype(F32)
    bands = 1e-4 + bidx * ((fb - 1 - 1e-4) / (fb - 1))
    ang = bands * wt + jnp.where(lane > fb, 0.5 * math.pi, 0.0)
    z = jnp.where(lane == 0, tt, jnp.where(lane <= 2 * fb, jnp.cos(ang), 0.0))
    h = jnp.sin(f1_ref[...] * (_dot(z, w1_ref[...], precision=hp) + b1_ref[...]))
    h = jnp.sin(f2_ref[...] * (_dot(h, w2_ref[...], precision=hp) + b2_ref[...]))
    h_hi = h.astype(MXU_DTYPE)
    h_lo = (h - h_hi.astype(F32)).astype(MXU_DTYPE)
    hk = jnp.concatenate([h_hi, h_hi, h_lo], axis=1)
    max_decay = math.log(DECAY_TARGET) / FAST_DECAY_PCT
    min_decay = math.log(DECAY_TARGET) / SLOW_DECAY_PCT
    cl = lax.broadcasted_iota(jnp.int32, (1, C), 1).astype(F32)
    deltas = jnp.abs(min_decay + cl * ((max_decay - min_decay) / (C - 1)))
    decay = jnp.where(m == L, 0.0, jnp.exp(-tt * deltas))
    decay2 = jnp.concatenate([decay, decay], axis=1)

    def emit(two):
        two_ref[0] = two[:, :C]
        two_ref[1] = two[:, C:]
        ss_ref[...] += jnp.sum(two * two, axis=0, keepdims=True)

    @pl.when(i == 0)
    def _():
        ss_ref[...] = jnp.zeros_like(ss_ref)
        lag0 = jnp.where(m == 0, 1.0, 0.0)
        emit(decay2 * (_dot(hk, w3_ref[0]) + lag0 * _dot(hk, w3_ref[1])))

    @pl.when(i != 0)
    def _():
        emit(decay2 * _dot(hk, w3_ref[(i * tmf >= L).astype(jnp.int32)]))


def _filter_call(L, C, w1, b1, f1, w2, b2, f2, w3):
    N = 2 * L
    fe, fh = w1.shape
    fb = (fe - 1) // 2
    assert fe <= LANES
    w1p = jnp.zeros((LANES, fh), F32).at[:fe].set(w1)
    tmf = _pick(L, 512)
    w3d = jnp.transpose(w3.reshape(fh, 2, 2, C), (2, 0, 1, 3)).reshape(2, fh, 2 * C)
    w3_hi = w3d.astype(MXU_DTYPE)
    w3_lo = (w3d - w3_hi.astype(F32)).astype(MXU_DTYPE)
    w3k = jnp.concatenate([w3_hi, w3_lo, w3_hi], axis=1)
    full = lambda a: pl.BlockSpec(a.shape, lambda i: (0,) * a.ndim)
    args = (w1p, b1.reshape(1, fh), f1.reshape(1, fh), w2, b2.reshape(1, fh), f2.reshape(1, fh), w3k)
    two, ss = pl.pallas_call(
        functools.partial(_filter_kernel, L=L, C=C, fb=fb, tmf=tmf),
        grid=(N // tmf,),
        in_specs=[full(a) for a in args],
        out_specs=[pl.BlockSpec((2, tmf, C), lambda i: (0, i, 0)),
                   pl.BlockSpec((1, 2 * C), lambda i: (0, 0))],
        out_shape=[jax.ShapeDtypeStruct((2, N, C), F32),
                   jax.ShapeDtypeStruct((1, 2 * C), F32)],
        compiler_params=_params("arbitrary"),
        name="filter",
    )(*args)
    return two, ss.reshape(2, 1, C)


def _stack(mr, mi):
    return jnp.concatenate([jnp.concatenate([mr, -mi], axis=-1),
                            jnp.concatenate([mi, mr], axis=-1)], axis=-2)


def _il_rows(m):
    n = m.shape[-2] // 2
    return jnp.stack([m[..., :n, :], m[..., n:, :]], axis=-2).reshape(m.shape)


def _il_cols(m):
    return jnp.swapaxes(_il_rows(jnp.swapaxes(m, -1, -2)), -1, -2)


def _fft_tables(nc):
    N = nc * NF
    na = nc // 2
    i32 = jnp.int32
    k1 = jnp.arange(nc, dtype=i32)
    b = jnp.arange(NF, dtype=i32)
    ang = (-2.0 * math.pi / nc) * ((k1[:, None] * k1[None, :]) % nc).astype(F32)
    fr, fi = jnp.cos(ang)[None], jnp.sin(ang)[None]
    ang = (-2.0 * math.pi / N) * (b[:, None] * k1[None, :]).astype(F32)
    tr, ti = jnp.cos(ang)[:, :, None], jnp.sin(ang)[:, :, None]
    ar, ai = tr * fr - ti * fi, tr * fi + ti * fr
    a_data = _il_rows(_stack(ar[:, :, :na], ai[:, :, :na]))
    a_filt = _il_rows(jnp.concatenate([ar, ai], axis=1))
    cr, ci = jnp.swapaxes(ar, 1, 2)[:, :na] / nc, -jnp.swapaxes(ai, 1, 2)[:, :na] / nc
    c_inv = _il_cols(_stack(cr, ci))
    ang = (-2.0 * math.pi / NF) * ((b[:, None] * b[None, :]) % NF).astype(F32)
    gr, gi = jnp.cos(ang), jnp.sin(ang)
    g_fwd = _il_cols(_stack(gr, gi))
    g_inv = _il_rows(_stack(gr.T / NF, -gi.T / NF))
    cast = lambda t: t.astype(MXU_DTYPE)
    return cast(a_data), cast(a_filt), cast(g_fwd), cast(g_inv), cast(c_inv)


NFP = NF + 8
U32 = jnp.uint32
PACKED = True


def _fine_per_step(nc):
    return max(8, min(32, (16 * 128) // nc))


def _rows2d(ref):
    return ref.reshape(math.prod(ref.shape[:-1]), ref.shape[-1])


def _ld_rows(ref, start, n, stride):
    return _rows2d(ref)[pl.ds(start, n, stride=stride), :]


def _st_rows(ref, start, n, stride, val):
    _rows2d(ref)[pl.ds(start, n, stride=stride), :] = val


def _spec_scratch(nc):
    if PACKED:
        return pltpu.VMEM((2, 1, nc * NFP, LANES), U32)
    return pltpu.VMEM((2, 2, nc * NFP, LANES), F32)


def _spec_store(y_ref, rows, v):
    if PACKED:
        w = pltpu.bitcast(v.astype(jnp.bfloat16), U32)
        y_ref[0, 0, rows, :] = w[:, :LANES]
        y_ref[1, 0, rows, :] = w[:, LANES:]
    else:
        for half in range(2):
            lanes = slice(half * LANES, (half + 1) * LANES)
            y_ref[half, 0, rows, :] = v[0::2, lanes]
            y_ref[half, 1, rows, :] = v[1::2, lanes]


def _spec_load(y_ref, rows):
    if PACKED:
        w = jnp.concatenate([y_ref[0, 0, rows, :], y_ref[1, 0, rows, :]], axis=1)
        return pltpu.bitcast(w, jnp.bfloat16)
    re = jnp.concatenate([y_ref[0, 0, rows, :], y_ref[1, 0, rows, :]], axis=1)
    im = jnp.concatenate([y_ref[0, 1, rows, :], y_ref[1, 1, rows, :]], axis=1)
    return jnp.stack([re, im], axis=1).reshape(2 * re.shape[0], re.shape[1]).astype(MXU_DTYPE)


def _phase_a(z_refs, at_ref, y_ref, step, *, planes, na, nc):
    bb = at_ref.shape[0]
    for b in range(bb):
        x = jnp.concatenate(
            [jnp.concatenate([_ld_rows(z, p * na * bb + b, na, bb) for p in range(planes)], axis=0)
             for z in z_refs], axis=1)
        r = _dot(at_ref[b], x.astype(MXU_DTYPE))
        _spec_store(y_ref, pl.ds(step * bb + b, nc, stride=NFP), r)


def _k1_rows(step, kk, k):
    return pl.ds(pl.multiple_of((step * kk + k) * NFP, 8), NF)


def _lconv_kernel(z0_ref, z1_ref, gate_ref, at_ref, ct_ref, gf_ref, gi_ref, hr_ref, hi_ref,
                  o_ref, y_ref, yc_ref, *, na, nc, kk, sa, sb):
    s = pl.program_id(2)

    @pl.when(s < sa)
    def _():
        _phase_a((z0_ref, z1_ref), at_ref, y_ref, s, planes=2, na=na, nc=nc)

    @pl.when((s >= sa) & (s < sa + sb))
    def _():
        for k in range(kk):
            rows = _k1_rows(s - sa, kk, k)
            z = _dot(gf_ref[...], _spec_load(y_ref, rows))
            zr, zi = z[:NF], z[NF:]
            hr, hi = hr_ref[k].astype(F32), hi_ref[k].astype(F32)
            x2 = jnp.concatenate([zr * hr - zi * hi, zr * hi + zi * hr], axis=0).astype(MXU_DTYPE)
            u = _dot(gi_ref[...], x2)
            _spec_store(y_ref, rows, u)

    @pl.when(s >= sa + sb)
    def _():
        step = s - sa - sb
        bb, ycp = ct_ref.shape[0], yc_ref.shape[2]
        for b in range(bb):
            x = _spec_load(y_ref, pl.ds(step * bb + b, nc, stride=NFP))
            r = _dot(ct_ref[b], x)
            for half in range(2):
                _st_rows(yc_ref.at[half], b, 2 * na, ycp, r[:, half * LANES:(half + 1) * LANES])
        for half in range(2):
            lanes = slice(half * LANES, (half + 1) * LANES)
            y = yc_ref[half, :, :bb, :].reshape(2, na, bb, LANES)
            o_ref[:, :, :, lanes] = (gate_ref[:, :, :, lanes] * y).astype(o_ref.dtype)


def _lconv_call(z5, z_off, gate5, gate_off, spec, order, tabs, na, out_dtype):
    a_data, _, g_fwd, g_inv, c_inv = tabs
    P = z5.shape[0]
    C = spec[0].shape[-1]
    nc = 2 * na
    ct = 2 * LANES
    kk = _pick(nc, 16, 1)
    BB = _fine_per_step(nc)
    YCP = BB + 8
    sa, sb = NF // BB, nc // kk
    zo, go = z_off // LANES, gate_off // ct
    ia = lambda s: jnp.minimum(s, sa - 1)
    ib = lambda s: jnp.clip(s - sa, 0, sb - 1)
    ic = lambda s: jnp.maximum(s - sa - sb, 0)
    zspec = lambda h: pl.BlockSpec((None, 2, na, BB, LANES), lambda p, jc, s: (p, 0, 0, ia(s), zo + 2 * jc + h))
    hspec = pl.BlockSpec((None, kk, NF, ct), lambda p, jc, s: (order, ib(s), 0, jc))
    gspec = pl.BlockSpec((2 * NF, 2 * NF), lambda p, jc, s: (0, 0))
    return pl.pallas_call(
        functools.partial(_lconv_kernel, na=na, nc=nc, kk=kk, sa=sa, sb=sb),
        grid=(P, C // ct, 2 * sa + sb),
        in_specs=[zspec(0), zspec(1),
                  pl.BlockSpec((None, 2, na, BB, ct), lambda p, jc, s: (p, 0, 0, ic(s), go + jc)),
                  pl.BlockSpec((BB, 2 * nc, nc), lambda p, jc, s: (ia(s), 0, 0)),
                  pl.BlockSpec((BB, nc, 2 * nc), lambda p, jc, s: (ic(s), 0, 0)),
                  gspec, gspec, hspec, hspec],
        out_specs=pl.BlockSpec((None, 2, na, BB, ct), lambda p, jc, s: (p, 0, 0, ic(s), jc)),
        out_shape=jax.ShapeDtypeStruct((P, 2, na, NF, C), out_dtype),
        scratch_shapes=[_spec_scratch(nc), pltpu.VMEM((2, 2 * na, YCP, LANES), F32)],
        compiler_params=_params("parallel", "parallel", "arbitrary"),
        name="lconv",
    )(z5, z5, gate5, a_data, c_inv, g_fwd, g_inv, spec[0], spec[1])


def _fspec_kernel(t0_ref, t1_ref, at_ref, gf_ref, ss_ref, bias_ref, hr_ref, hi_ref, y_ref,
                  *, nc, kk, sa):
    s = pl.program_id(2)

    @pl.when(s < sa)
    def _():
        _phase_a((t0_ref, t1_ref), at_ref, y_ref, s, planes=1, na=nc, nc=nc)

    @pl.when(s >= sa)
    def _():
        scale = lax.rsqrt(ss_ref[...] + EPS)
        for k in range(kk):
            z = _dot(gf_ref[...], _spec_load(y_ref, _k1_rows(s - sa, kk, k)))
            hr_ref[k] = (z[:NF] * scale + bias_ref[...]).astype(hr_ref.dtype)
            hi_ref[k] = (z[NF:] * scale).astype(hi_ref.dtype)


def _fspec_call(two, ss, bias, tabs, nc):
    _, a_filt, g_fwd, _, _ = tabs
    C = two.shape[-1]
    ct = 2 * LANES
    kk = _pick(nc, 16, 1)
    BB = _fine_per_step(nc)
    sa, sb = NF // BB, nc // kk
    two5 = two.reshape(2, 1, nc, NF, C)
    ia = lambda s: jnp.minimum(s, sa - 1)
    ib = lambda s: jnp.maximum(s - sa, 0)
    tspec = lambda h: pl.BlockSpec((None, 1, nc, BB, LANES), lambda o, jc, s: (o, 0, 0, ia(s), 2 * jc + h))
    vspec = pl.BlockSpec((None, 1, ct), lambda o, jc, s: (o, 0, jc))
    hspec = pl.BlockSpec((None, kk, NF, ct), lambda o, jc, s: (o, ib(s), 0, jc))
    out = jax.ShapeDtypeStruct((2, nc, NF, C), MXU_DTYPE)
    return pl.pallas_call(
        functools.partial(_fspec_kernel, nc=nc, kk=kk, sa=sa),
        grid=(2, C // ct, sa + sb),
        in_specs=[tspec(0), tspec(1),
                  pl.BlockSpec((BB, 2 * nc, nc), lambda o, jc, s: (ia(s), 0, 0)),
                  pl.BlockSpec((2 * NF, 2 * NF), lambda o, jc, s: (0, 0)),
                  vspec, vspec],
        out_specs=[hspec, hspec],
        out_shape=[out, out],
        scratch_shapes=[_spec_scratch(nc)],
        compiler_params=_params("parallel", "parallel", "arbitrary"),
        name="fspec",
    )(two5, two5, a_filt, g_fwd, ss, bias)


def _sigmoid(x):
    return 0.5 * (jnp.tanh(0.5 * x) + 1.0)


def _merge_kernel(h_ref, a_ref, b_ref, x_ref, wga_ref, wgb_ref, wa_ref, wb_ref, wo_ref, o_ref,
                  m0_ref, m1_ref, acc_ref, *, nj):
    q = pl.program_id(0)
    first = (jnp.maximum(q - 1, 0) % nj) == 0

    @pl.when(q == 0)
    def _():
        m1_ref[...] = jnp.zeros_like(m1_ref)
        acc_ref[...] = jnp.zeros_like(acc_ref)

    def body(m_new_ref, m_old_ref):
        contrib = _dot(m_old_ref[...], wo_ref[...])
        acc = jnp.where(first, x_ref[...], acc_ref[...]) + contrib
        acc_ref[...] = acc
        o_ref[...] = acc
        h = h_ref[...]
        ga = _sigmoid(_dot(h, wga_ref[...]))
        gb = _sigmoid(_dot(h, wgb_ref[...]))
        m = ga * _dot(a_ref[...], wa_ref[...]) + gb * _dot(b_ref[...], wb_ref[...])
        m_new_ref[...] = m.astype(m_new_ref.dtype)

    @pl.when(q % 2 == 0)
    def _():
        body(m0_ref, m1_ref)

    @pl.when(q % 2 == 1)
    def _():
        body(m1_ref, m0_ref)


def _merge_call(h, a, b, x2, w_ga, w_gb, wa, wb, wo):
    T, D = x2.shape
    Dp, C = a.shape[1], b.shape[1]
    tm = _pick(T, 512)
    tn = _pick(D, 512, LANES)
    nj = D // tn
    Q = (T // tm) * nj
    cur = lambda q: jnp.minimum(q, Q - 1)
    pend = lambda q: jnp.maximum(q - 1, 0)
    row = lambda f: (lambda q: (f(q) // nj, 0))
    col = lambda q: (0, cur(q) % nj)
    return pl.pallas_call(
        functools.partial(_merge_kernel, nj=nj),
        grid=(Q + 1,),
        in_specs=[pl.BlockSpec((tm, D), row(cur)),
                  pl.BlockSpec((tm, Dp), row(cur)),
                  pl.BlockSpec((tm, C), row(cur)),
                  pl.BlockSpec((tm, D), row(pend)),
                  pl.BlockSpec((D, tn), col),
                  pl.BlockSpec((D, tn), col),
                  pl.BlockSpec((Dp, tn), col),
                  pl.BlockSpec((C, tn), col),
                  pl.BlockSpec((tn, D), lambda q: (pend(q) % nj, 0))],
        out_specs=pl.BlockSpec((tm, D), row(pend)),
        out_shape=jax.ShapeDtypeStruct((T, D), F32),
        scratch_shapes=[pltpu.VMEM((tm, tn), MXU_DTYPE), pltpu.VMEM((tm, tn), MXU_DTYPE),
                        pltpu.VMEM((tm, D), F32)],
        compiler_params=_params("arbitrary"),
        name="merge",
    )(h, a, b, x2, w_ga, w_gb, wa, wb, wo)


def _ffn_kernel(x_ref, g_ref, wg_ref, wu_ref, wd_ref, gf_ref, o_ref, h_ref):
    j = pl.program_id(1)

    @pl.when(j == 0)
    def _():
        x = x_ref[...]
        h_ref[...] = (_rms(x) * g_ref[...]).astype(h_ref.dtype)
        o_ref[...] = x

    h = h_ref[...]
    act = jax.nn.silu(_dot(h, wg_ref[...])) * _dot(h, wu_ref[...])
    o_ref[...] += _dot(act.astype(MXU_DTYPE), wd_ref[...])

    @pl.when(j == pl.num_programs(1) - 1)
    def _():
        o_ref[...] = _rms(o_ref[...]) * gf_ref[...]


def _ffn_call(x2, g, wg, wu, wd, gf):
    T, D = x2.shape
    Dff = wg.shape[1]
    tm = _pick(T, 1024)
    tf = _pick(Dff, 512, LANES)
    return pl.pallas_call(
        _ffn_kernel,
        grid=(T // tm, Dff // tf),
        in_specs=[pl.BlockSpec((tm, D), lambda i, j: (i, 0)),
                  pl.BlockSpec((1, D), lambda i, j: (0, 0)),
                  pl.BlockSpec((D, tf), lambda i, j: (0, j)),
                  pl.BlockSpec((D, tf), lambda i, j: (0, j)),
                  pl.BlockSpec((tf, D), lambda i, j: (j, 0)),
                  pl.BlockSpec((1, D), lambda i, j: (0, 0))],
        out_specs=pl.BlockSpec((tm, D), lambda i, j: (i, 0)),
        out_shape=jax.ShapeDtypeStruct((T, D), F32),
        scratch_shapes=[pltpu.VMEM((tm, D), MXU_DTYPE)],
        compiler_params=_params("parallel", "arbitrary"),
        name="ffn",
    )(x2, g.reshape(1, D), wg, wu, wd, gf.reshape(1, D))


def _trunk(x, w):
    B, L, D = x.shape
    T = B * L
    C = w["hyena_bias"].shape[1]
    assert B % 2 == 0 and L % NF == 0 and C % (2 * LANES) == 0 and w["hyena_bias"].shape[0] == 2
    na = L // NF
    nc = 2 * na
    P = B // 2
    x2 = x.reshape(T, D)

    a, h = _pool_call(x2, w["g_mix"], w["w_pool"], w["pool_w"], w["pool_scale"], L)
    uc = _hyproj_call(h, w["w_hy"], w["conv_w"], w["conv_b"], L, C)
    uc5 = uc.reshape(P, 2, na, NF, 3 * C)

    tabs = _fft_tables(nc)
    two, ss = _filter_call(L, C, *w["filt"])
    spec = _fspec_call(two, ss, w["hyena_bias"].reshape(2, 1, C), tabs, nc)

    z1 = _lconv_call(uc5, 0, uc5, C, spec, 0, tabs, na, F32)
    b = _lconv_call(z1, 0, uc5, 2 * C, spec, 1, tabs, na, MXU_DTYPE).reshape(T, C)

    xn = _merge_call(h, a, b, x2, w["w_ga"], w["w_gb"], w["wa"], w["wb"], w["wo"])
    y = _ffn_call(xn, w["g_ffn"], w["wg"], w["wu"], w["wd"], w["g_final"])
    return y.reshape(B, L, D)


def kernel(x_prompt, x_sample, g_mix, w_in, pool_w, pool_scale, conv_w, conv_b, filt_w1, filt_b1, filt_freq1, filt_w2, filt_b2, filt_freq2, filt_w3, hyena_bias, w_branch_a, w_branch_b, w_out, g_ffn, w_gate, w_up, w_down, g_final):
    assert g_mix.shape[0] == 1, "depth-1 block"
    D = x_prompt.shape[-1]
    Dp = pool_scale.shape[1]
    C = hyena_bias.shape[2]
    s1, s2, s3 = Dp, Dp + 3 * C, Dp + 3 * C + D
    cast = lambda t: t.astype(MXU_DTYPE)
    w = {
        "g_mix": g_mix[0], "w_pool": cast(w_in[0][:, :s1]), "w_hy": cast(w_in[0][:, s1:s2]),
        "w_ga": cast(w_in[0][:, s2:s3]), "w_gb": cast(w_in[0][:, s3:]),
        "pool_w": cast(pool_w[0]), "pool_scale": pool_scale[0],
        "conv_w": conv_w[0], "conv_b": conv_b[0],
        "filt": (filt_w1[0], filt_b1[0], filt_freq1[0], filt_w2[0], filt_b2[0], filt_freq2[0], filt_w3[0]),
        "hyena_bias": hyena_bias[0],
        "wa": cast(w_branch_a[0]), "wb": cast(w_branch_b[0]), "wo": cast(w_out[0]),
        "g_ffn": g_ffn[0], "wg": cast(w_gate[0]), "wu": cast(w_up[0]), "wd": cast(w_down[0]),
        "g_final": g_final,
    }
    return (_trunk(x_prompt, w), _trunk(x_sample, w))
```

```python
import functools
import math

import jax
import jax.numpy as jnp
from jax import lax
from jax.experimental import pallas as pl
from jax.experimental.pallas import tpu as pltpu

F32 = jnp.float32
MXU_DTYPE = jnp.bfloat16

EPS = 1e-6
POOL_WINDOWS = (2, 4, 8, 16)
DECAY_TARGET = 1e-2
FAST_DECAY_PCT = 0.3
SLOW_DECAY_PCT = 1.5

LANES = 128
NF = 128
HALO = 16
VMEM_LIMIT = 56 * 1024 * 1024


def _pick(n, pref, mult=8):
    best = None
    for d in range(mult, min(n, pref) + 1, mult):
        if n % d == 0:
            best = d
    return best if best is not None else n


def _params(*sem):
    return pltpu.CompilerParams(dimension_semantics=sem, vmem_limit_bytes=VMEM_LIMIT)


def _dot(a, b, **kw):
    return jnp.dot(a, b, preferred_element_type=F32, **kw)


def _rms(x):
    return x * lax.rsqrt(jnp.mean(x * x, axis=-1, keepdims=True) + EPS)


def _halo_specs(tm, T, D, tile_of):
    r = tm // HALO
    last = T // HALO - 1
    return [
        pl.BlockSpec((HALO, D), lambda *g: (jnp.maximum(tile_of(*g) * r - 1, 0), 0)),
        pl.BlockSpec((tm, D), lambda *g: (tile_of(*g), 0)),
        pl.BlockSpec((HALO, D), lambda *g: (jnp.minimum((tile_of(*g) + 1) * r, last), 0)),
    ]


def _halo_keep(t0, tm, L):
    return t0 > 0, t0 + tm < L


def _fill_hcat(hcat_ref, hp_ref, hm_ref, hn_ref, tm, t0, L):
    keep_prev, keep_next = _halo_keep(t0, tm, L)
    hcat_ref[pl.ds(0, HALO), :] = jnp.where(keep_prev, hp_ref[...], jnp.zeros_like(hp_ref))
    hcat_ref[pl.ds(HALO, tm), :] = hm_ref[...]
    hcat_ref[pl.ds(HALO + tm, HALO), :] = jnp.where(keep_next, hn_ref[...], jnp.zeros_like(hn_ref))


def _pool_chunks(tm):
    rows = tm + 2 * HALO
    ch = min(tm, 256)
    kw = min(rows, 512)
    return ch, kw, [(c * ch, min(c * ch, rows - kw)) for c in range(tm // ch)]


def _pool_kernel(xp_ref, xm_ref, xn_ref, g_ref, w_ref, pw_ref, ps_ref, o_ref, h_ref, hcat_ref, band_ref,
                 *, tm, L):
    rows = tm + 2 * HALO
    gw = pw_ref.shape[-1]
    ch, kw, chunks = _pool_chunks(tm)

    @pl.when(pl.program_id(0) == 0)
    def _():
        r = lax.broadcasted_iota(jnp.int32, (ch, kw), 0)
        s = lax.broadcasted_iota(jnp.int32, (ch, kw), 1)
        for g, w in enumerate(POOL_WINDOWS):
            for c, (r0, s0) in enumerate(chunks):
                d = (s + (s0 - HALO)) - (r + r0)
                band_ref[g, c] = jnp.where((d >= -(w // 2)) & (d < w - w // 2), 1.0, 0.0).astype(MXU_DTYPE)

    t0 = (pl.program_id(0) * tm) % L
    keep_prev, keep_next = _halo_keep(t0, tm, L)
    for x_ref, row0, n, keep in ((xp_ref, 0, HALO, keep_prev), (xm_ref, HALO, tm, True),
                                 (xn_ref, HALO + tm, HALO, keep_next)):
        hx = (_rms(x_ref[...]) * g_ref[...]).astype(hcat_ref.dtype)
        hcat_ref[pl.ds(row0, n), :] = hx if keep is True else jnp.where(keep, hx, jnp.zeros_like(hx))
    h_ref[...] = hcat_ref[pl.ds(HALO, tm), :]
    p = _dot(hcat_ref[...], w_ref[...])
    p_hi = p.astype(MXU_DTYPE)
    p_lo = (p - p_hi.astype(F32)).astype(MXU_DTYPE)
    tc = t0 + lax.broadcasted_iota(jnp.int32, (tm, 1), 0)
    for g, w in enumerate(POOL_WINDOWS):
        lo_off, hi_off = w // 2, w - w // 2
        sl = slice(g * gw, (g + 1) * gw)
        wsum = jnp.concatenate(
            [_dot(band_ref[g, c], p_hi[s0:s0 + kw, sl]) + _dot(band_ref[g, c], p_lo[s0:s0 + kw, sl])
             for c, (_, s0) in enumerate(chunks)], axis=0)
        cnt = (jnp.minimum(tc + hi_off, L) - jnp.maximum(tc - lo_off, 0)).astype(F32)
        pooled = wsum / cnt - p[HALO:HALO + tm, sl]
        mixed = _dot(pooled.astype(MXU_DTYPE), pw_ref[g]) * ps_ref[:, sl]
        o_ref[:, sl] = mixed.astype(o_ref.dtype)


def _pool_call(x2, g, w_pool, pool_w, pool_scale, L):
    T, D = x2.shape
    Dp = w_pool.shape[1]
    G, gw, _ = pool_w.shape
    tm = _pick(L, 1024, HALO)
    ch, kw, chunks = _pool_chunks(tm)
    return pl.pallas_call(
        functools.partial(_pool_kernel, tm=tm, L=L),
        grid=(T // tm,),
        in_specs=_halo_specs(tm, T, D, lambda i: i) + [
            pl.BlockSpec((1, D), lambda i: (0, 0)),
            pl.BlockSpec((D, Dp), lambda i: (0, 0)),
            pl.BlockSpec((G, gw, gw), lambda i: (0, 0, 0)),
            pl.BlockSpec((1, Dp), lambda i: (0, 0)),
        ],
        out_specs=[pl.BlockSpec((tm, Dp), lambda i: (i, 0)),
                   pl.BlockSpec((tm, D), lambda i: (i, 0))],
        out_shape=[jax.ShapeDtypeStruct((T, Dp), MXU_DTYPE),
                   jax.ShapeDtypeStruct((T, D), MXU_DTYPE)],
        scratch_shapes=[pltpu.VMEM((tm + 2 * HALO, D), MXU_DTYPE),
                        pltpu.VMEM((len(POOL_WINDOWS), len(chunks), ch, kw), MXU_DTYPE)],
        compiler_params=_params("arbitrary"),
        name="pool",
    )(x2, x2, x2, g.reshape(1, D), w_pool, pool_w, pool_scale.reshape(1, Dp))


def _hyproj_kernel(hp_ref, hm_ref, hn_ref, w_ref, cw_ref, cb_ref, o_ref, hcat_ref, *, tm, L):
    rows = tm + 2 * HALO
    t0 = (pl.program_id(0) * tm) % L

    @pl.when(pl.program_id(1) == 0)
    def _():
        _fill_hcat(hcat_ref, hp_ref, hm_ref, hn_ref, tm, t0, L)

    p = _dot(hcat_ref[...], w_ref[...])
    prev = pltpu.roll(p, 1, 0)
    nxt = pltpu.roll(p, rows - 1, 0)
    y = prev * cw_ref[0:1, :] + p * cw_ref[1:2, :] + nxt * cw_ref[2:3, :] + cb_ref[...]
    o_ref[...] = y[HALO:HALO + tm]


def _hyproj_call(h, w_hy, conv_w, conv_b, L, C):
    T, D = h.shape
    n3 = w_hy.shape[1]
    tn = C if C % LANES == 0 else n3
    tm = _pick(L, 1024, HALO)
    return pl.pallas_call(
        functools.partial(_hyproj_kernel, tm=tm, L=L),
        grid=(T // tm, n3 // tn),
        in_specs=_halo_specs(tm, T, D, lambda i, j: i) + [
            pl.BlockSpec((D, tn), lambda i, j: (0, j)),
            pl.BlockSpec((conv_w.shape[0], tn), lambda i, j: (0, j)),
            pl.BlockSpec((1, tn), lambda i, j: (0, j)),
        ],
        out_specs=pl.BlockSpec((tm, tn), lambda i, j: (i, j)),
        out_shape=jax.ShapeDtypeStruct((T, n3), F32),
        scratch_shapes=[pltpu.VMEM((tm + 2 * HALO, D), MXU_DTYPE)],
        compiler_params=_params("parallel", "arbitrary"),
        name="hyproj",
    )(h, h, h, w_hy, conv_w, conv_b.reshape(1, n3))


def _filter_kernel(w1_ref, b1_ref, f1_ref, w2_ref, b2_ref, f2_ref, w3_ref, two_ref, ss_ref,
                   *, L, C, fb, tmf):
    i = pl.program_id(0)
    N = 2 * L
    hp = lax.Precision.HIGHEST
    m = i * tmf + lax.broadcasted_iota(jnp.int32, (tmf, 1), 0)
    tf = jnp.where(m < L, m, N - m).astype(F32)
    tt = tf / (L - 1)
    wt = (2.0 * math.pi) * tf / L
    lane = lax.broadcasted_iota(jnp.int32, (1, LANES), 1)
    bidx = jnp.where(lane <= fb, lane - 1, lane - 1 - fb).astype(F32)
    bands = 1e-4 + bidx * ((fb - 1 - 1e-4) / (fb - 1))
    ang = bands * wt + jnp.where(lane > fb, 0.5 * math.pi, 0.0)
    z = jnp.where(lane == 0, tt, jnp.where(lane <= 2 * fb, jnp.cos(ang), 0.0))
    h = jnp.sin(f1_ref[...] * (_dot(z, w1_ref[...], precision=hp) + b1_ref[...]))
    h = jnp.sin(f2_ref[...] * (_dot(h, w2_ref[...], precision=hp) + b2_ref[...]))
    h_hi = h.astype(MXU_DTYPE)
    h_lo = (h - h_hi.astype(F32)).astype(MXU_DTYPE)
    hk = jnp.concatenate([h_hi, h_hi, h_lo], axis=1)
    max_decay = math.log(DECAY_TARGET) / FAST_DECAY_PCT
    min_decay = math.log(DECAY_TARGET) / SLOW_DECAY_PCT
    cl = lax.broadcasted_iota(jnp.int32, (1, C), 1).astype(F32)
    deltas = jnp.abs(min_decay + cl * ((max_decay - min_decay) / (C - 1)))
    decay = jnp.where(m == L, 0.0, jnp.exp(-tt * deltas))
    decay2 = jnp.concatenate([decay, decay], axis=1)

    def emit(two):
        two_ref[0] = two[:, :C]
        two_ref[1] = two[:, C:]
        ss_ref[...] += jnp.sum(two * two, axis=0, keepdims=True)

    @pl.when(i == 0)
    def _():
        ss_ref[...] = jnp.zeros_like(ss_ref)
        lag0 = jnp.where(m == 0, 1.0, 0.0)
        emit(decay2 * (_dot(hk, w3_ref[0]) + lag0 * _dot(hk, w3_ref[1])))

    @pl.when(i != 0)
    def _():
        emit(decay2 * _dot(hk, w3_ref[(i * tmf >= L).astype(jnp.int32)]))


def _filter_call(L, C, w1, b1, f1, w2, b2, f2, w3):
    N = 2 * L
    fe, fh = w1.shape
    fb = (fe - 1) // 2
    assert fe <= LANES
    w1p = jnp.zeros((LANES, fh), F32).at[:fe].set(w1)
    tmf = _pick(L, 512)
    w3d = jnp.transpose(w3.reshape(fh, 2, 2, C), (2, 0, 1, 3)).reshape(2, fh, 2 * C)
    w3_hi = w3d.astype(MXU_DTYPE)
    w3_lo = (w3d - w3_hi.astype(F32)).astype(MXU_DTYPE)
    w3k = jnp.concatenate([w3_hi, w3_lo, w3_hi], axis=1)
    full = lambda a: pl.BlockSpec(a.shape, lambda i: (0,) * a.ndim)
    args = (w1p, b1.reshape(1, fh), f1.reshape(1, fh), w2, b2.reshape(1, fh), f2.reshape(1, fh), w3k)
    two, ss = pl.pallas_call(
        functools.partial(_filter_kernel, L=L, C=C, fb=fb, tmf=tmf),
        grid=(N // tmf,),
        in_specs=[full(a) for a in args],
        out_specs=[pl.BlockSpec((2, tmf, C), lambda i: (0, i, 0)),
                   pl.BlockSpec((1, 2 * C), lambda i: (0, 0))],
        out_shape=[jax.ShapeDtypeStruct((2, N, C), F32),
                   jax.ShapeDtypeStruct((1, 2 * C), F32)],
        compiler_params=_params("arbitrary"),
        name="filter",
    )(*args)
    return two, ss.reshape(2, 1, C)


def _stack(mr, mi):
    return jnp.concatenate([jnp.concatenate([mr, -mi], axis=-1),
                            jnp.concatenate([mi, mr], axis=-1)], axis=-2)


def _il_rows(m):
    n = m.shape[-2] // 2
    return jnp.stack([m[..., :n, :], m[..., n:, :]], axis=-2).reshape(m.shape)


def _il_cols(m):
    return jnp.swapaxes(_il_rows(jnp.swapaxes(m, -1, -2)), -1, -2)


def _fft_tables(nc):
    N = nc * NF
    na = nc // 2
    i32 = jnp.int32
    k1 = jnp.arange(nc, dtype=i32)
    b = jnp.arange(NF, dtype=i32)
    ang = (-2.0 * math.pi / nc) * ((k1[:, None] * k1[None, :]) % nc).astype(F32)
    fr, fi = jnp.cos(ang)[None], jnp.sin(ang)[None]
    ang = (-2.0 * math.pi / N) * (b[:, None] * k1[None, :]).astype(F32)
    tr, ti = jnp.cos(ang)[:, :, None], jnp.sin(ang)[:, :, None]
    ar, ai = tr * fr - ti * fi, tr * fi + ti * fr
    a_data = _il_rows(_stack(ar[:, :, :na], ai[:, :, :na]))
    a_filt = _il_rows(jnp.concatenate([ar, ai], axis=1))
    cr, ci = jnp.swapaxes(ar, 1, 2)[:, :na] / nc, -jnp.swapaxes(ai, 1, 2)[:, :na] / nc
    c_inv = _il_cols(_stack(cr, ci))
    ang = (-2.0 * math.pi / NF) * ((b[:, None] * b[None, :]) % NF).astype(F32)
    gr, gi = jnp.cos(ang), jnp.sin(ang)
    g_fwd = _il_cols(_stack(gr, gi))
    g_inv = _il_rows(_stack(gr.T / NF, -gi.T / NF))
    cast = lambda t: t.astype(MXU_DTYPE)
    return cast(a_data), cast(a_filt), cast(g_fwd), cast(g_inv), cast(c_inv)


NFP = NF + 8
U32 = jnp.uint32
PACKED = True


def _fine_per_step(nc):
    return 16 if nc > 64 else 64


def _rows2d(ref):
    return ref.reshape(math.prod(ref.shape[:-1]), ref.shape[-1])


def _ld_rows(ref, start, n, stride):
    return _rows2d(ref)[pl.ds(start, n, stride=stride), :]


def _st_rows(ref, start, n, stride, val):
    _rows2d(ref)[pl.ds(start, n, stride=stride), :] = val


def _spec_scratch(nc):
    if PACKED:
        return pltpu.VMEM((2, 1, nc * NFP, LANES), U32)
    return pltpu.VMEM((2, 2, nc * NFP, LANES), F32)


def _spec_store(y_ref, rows, v):
    if PACKED:
        w = pltpu.bitcast(v.astype(jnp.bfloat16), U32)
        y_ref[0, 0, rows, :] = w[:, :LANES]
        y_ref[1, 0, rows, :] = w[:, LANES:]
    else:
        for half in range(2):
            lanes = slice(half * LANES, (half + 1) * LANES)
            y_ref[half, 0, rows, :] = v[0::2, lanes]
            y_ref[half, 1, rows, :] = v[1::2, lanes]


def _spec_load(y_ref, rows):
    if PACKED:
        w = jnp.concatenate([y_ref[0, 0, rows, :], y_ref[1, 0, rows, :]], axis=1)
        return pltpu.bitcast(w, jnp.bfloat16)
    re = jnp.concatenate([y_ref[0, 0, rows, :], y_ref[1, 0, rows, :]], axis=1)
    im = jnp.concatenate([y_ref[0, 1, rows, :], y_ref[1, 1, rows, :]], axis=1)
    return jnp.stack([re, im], axis=1).reshape(2 * re.shape[0], re.shape[1]).astype(MXU_DTYPE)


def _phase_a(z_refs, at_ref, y_ref, step, *, planes, na, nc):
    bb = at_ref.shape[0]
    for b in range(bb):
        x = jnp.concatenate(
            [jnp.concatenate([_ld_rows(z, p * na * bb + b, na, bb) for p in range(planes)], axis=0)
             for z in z_refs], axis=1)
        r = _dot(at_ref[b], x.astype(MXU_DTYPE))
        _spec_store(y_ref, pl.ds(step * bb + b, nc, stride=NFP), r)


def _k1_rows(step, kk, k):
    return pl.ds(pl.multiple_of((step * kk + k) * NFP, 8), NF)


def _lconv_kernel(z0_ref, z1_ref, gate_ref, at_ref, ct_ref, gf_ref, gi_ref, hr_ref, hi_ref,
                  o_ref, y_ref, yc_ref, *, na, nc, kk, sa, sb):
    s = pl.program_id(2)

    @pl.when(s < sa)
    def _():
        _phase_a((z0_ref, z1_ref), at_ref, y_ref, s, planes=2, na=na, nc=nc)

    @pl.when((s >= sa) & (s < sa + sb))
    def _():
        for k in range(kk):
            rows = _k1_rows(s - sa, kk, k)
            z = _dot(gf_ref[...], _spec_load(y_ref, rows))
            zr, zi = z[:NF], z[NF:]
            hr, hi = hr_ref[k].astype(F32), hi_ref[k].astype(F32)
            x2 = jnp.concatenate([zr * hr - zi * hi, zr * hi + zi * hr], axis=0).astype(MXU_DTYPE)
            u = _dot(gi_ref[...], x2)
            _spec_store(y_ref, rows, u)

    @pl.when(s >= sa + sb)
    def _():
        step = s - sa - sb
        bb, ycp = ct_ref.shape[0], yc_ref.shape[2]
        for b in range(bb):
            x = _spec_load(y_ref, pl.ds(step * bb + b, nc, stride=NFP))
            r = _dot(ct_ref[b], x)
            for half in range(2):
                _st_rows(yc_ref.at[half], b, 2 * na, ycp, r[:, half * LANES:(half + 1) * LANES])
        for half in range(2):
            lanes = slice(half * LANES, (half + 1) * LANES)
            y = yc_ref[half, :, :bb, :].reshape(2, na, bb, LANES)
            o_ref[:, :, :, lanes] = (gate_ref[:, :, :, lanes] * y).astype(o_ref.dtype)


def _lconv_call(z5, z_off, gate5, gate_off, spec, order, tabs, na, out_dtype):
    a_data, _, g_fwd, g_inv, c_inv = tabs
    P = z5.shape[0]
    C = spec[0].shape[-1]
    nc = 2 * na
    ct = 2 * LANES
    kk = _pick(nc, 32, 1)
    BB = _fine_per_step(nc)
    YCP = BB + 8
    sa, sb = NF // BB, nc // kk
    zo, go = z_off // LANES, gate_off // ct
    ia = lambda s: jnp.minimum(s, sa - 1)
    ib = lambda s: jnp.clip(s - sa, 0, sb - 1)
    ic = lambda s: jnp.maximum(s - sa - sb, 0)
    zspec = lambda h: pl.BlockSpec((None, 2, na, BB, LANES), lambda p, jc, s: (p, 0, 0, ia(s), zo + 2 * jc + h))
    hspec = pl.BlockSpec((None, kk, NF, ct), lambda p, jc, s: (order, ib(s), 0, jc))
    gspec = pl.BlockSpec((2 * NF, 2 * NF), lambda p, jc, s: (0, 0))
    return pl.pallas_call(
        functools.partial(_lconv_kernel, na=na, nc=nc, kk=kk, sa=sa, sb=sb),
        grid=(P, C // ct, 2 * sa + sb),
        in_specs=[zspec(0), zspec(1),
                  pl.BlockSpec((None, 2, na, BB, ct), lambda p, jc, s: (p, 0, 0, ic(s), go + jc)),
                  pl.BlockSpec((BB, 2 * nc, nc), lambda p, jc, s: (ia(s), 0, 0)),
                  pl.BlockSpec((BB, nc, 2 * nc), lambda p, jc, s: (ic(s), 0, 0)),
                  gspec, gspec, hspec, hspec],
        out_specs=pl.BlockSpec((None, 2, na, BB, ct), lambda p, jc, s: (p, 0, 0, ic(s), jc)),
        out_shape=jax.ShapeDtypeStruct((P, 2, na, NF, C), out_dtype),
        scratch_shapes=[_spec_scratch(nc), pltpu.VMEM((2, 2 * na, YCP, LANES), F32)],
        compiler_params=_params("parallel", "parallel", "arbitrary"),
        name="lconv",
    )(z5, z5, gate5, a_data, c_inv, g_fwd, g_inv, spec[0], spec[1])


def _fspec_kernel(t0_ref, t1_ref, at_ref, gf_ref, ss_ref, bias_ref, hr_ref, hi_ref, y_ref,
                  *, nc, kk, sa):
    s = pl.program_id(2)

    @pl.when(s < sa)
    def _():
        _phase_a((t0_ref, t1_ref), at_ref, y_ref, s, planes=1, na=nc, nc=nc)

    @pl.when(s >= sa)
    def _():
        scale = lax.rsqrt(ss_ref[...] + EPS)
        for k in range(kk):
            z = _dot(gf_ref[...], _spec_load(y_ref, _k1_rows(s - sa, kk, k)))
            hr_ref[k] = (z[:NF] * scale + bias_ref[...]).astype(hr_ref.dtype)
            hi_ref[k] = (z[NF:] * scale).astype(hi_ref.dtype)


def _fspec_call(two, ss, bias, tabs, nc):
    _, a_filt, g_fwd, _, _ = tabs
    C = two.shape[-1]
    ct = 2 * LANES
    kk = _pick(nc, 32, 1)
    BB = _fine_per_step(nc)
    sa, sb = NF // BB, nc // kk
    two5 = two.reshape(2, 1, nc, NF, C)
    ia = lambda s: jnp.minimum(s, sa - 1)
    ib = lambda s: jnp.maximum(s - sa, 0)
    tspec = lambda h: pl.BlockSpec((None, 1, nc, BB, LANES), lambda o, jc, s: (o, 0, 0, ia(s), 2 * jc + h))
    vspec = pl.BlockSpec((None, 1, ct), lambda o, jc, s: (o, 0, jc))
    hspec = pl.BlockSpec((None, kk, NF, ct), lambda o, jc, s: (o, ib(s), 0, jc))
    out = jax.ShapeDtypeStruct((2, nc, NF, C), MXU_DTYPE)
    return pl.pallas_call(
        functools.partial(_fspec_kernel, nc=nc, kk=kk, sa=sa),
        grid=(2, C // ct, sa + sb),
        in_specs=[tspec(0), tspec(1),
                  pl.BlockSpec((BB, 2 * nc, nc), lambda o, jc, s: (ia(s), 0, 0)),
                  pl.BlockSpec((2 * NF, 2 * NF), lambda o, jc, s: (0, 0)),
                  vspec, vspec],
        out_specs=[hspec, hspec],
        out_shape=[out, out],
        scratch_shapes=[_spec_scratch(nc)],
        compiler_params=_params("parallel", "parallel", "arbitrary"),
        name="fspec",
    )(two5, two5, a_filt, g_fwd, ss, bias)


def _sigmoid(x):
    return 0.5 * (jnp.tanh(0.5 * x) + 1.0)


def _merge_kernel(h_ref, a_ref, b_ref, x_ref, wga_ref, wgb_ref, wa_ref, wb_ref, wo_ref, o_ref,
                  m0_ref, m1_ref, acc_ref, *, nj):
    q = pl.program_id(0)
    first = (jnp.maximum(q - 1, 0) % nj) == 0

    @pl.when(q == 0)
    def _():
        m1_ref[...] = jnp.zeros_like(m1_ref)
        acc_ref[...] = jnp.zeros_like(acc_ref)

    def body(m_new_ref, m_old_ref):
        contrib = _dot(m_old_ref[...], wo_ref[...])
        acc = jnp.where(first, x_ref[...], acc_ref[...]) + contrib
        acc_ref[...] = acc
        o_ref[...] = acc
        h = h_ref[...]
        ga = _sigmoid(_dot(h, wga_ref[...]))
        gb = _sigmoid(_dot(h, wgb_ref[...]))
        m = ga * _dot(a_ref[...], wa_ref[...]) + gb * _dot(b_ref[...], wb_ref[...])
        m_new_ref[...] = m.astype(m_new_ref.dtype)

    @pl.when(q % 2 == 0)
    def _():
        body(m0_ref, m1_ref)

    @pl.when(q % 2 == 1)
    def _():
        body(m1_ref, m0_ref)


def _merge_call(h, a, b, x2, w_ga, w_gb, wa, wb, wo):
    T, D = x2.shape
    Dp, C = a.shape[1], b.shape[1]
    tm = _pick(T, 512)
    tn = _pick(D, 512, LANES)
    nj = D // tn
    Q = (T // tm) * nj
    cur = lambda q: jnp.minimum(q, Q - 1)
    pend = lambda q: jnp.maximum(q - 1, 0)
    row = lambda f: (lambda q: (f(q) // nj, 0))
    col = lambda q: (0, cur(q) % nj)
    return pl.pallas_call(
        functools.partial(_merge_kernel, nj=nj),
        grid=(Q + 1,),
        in_specs=[pl.BlockSpec((tm, D), row(cur)),
                  pl.BlockSpec((tm, Dp), row(cur)),
                  pl.BlockSpec((tm, C), row(cur)),
                  pl.BlockSpec((tm, D), row(pend)),
                  pl.BlockSpec((D, tn), col),
                  pl.BlockSpec((D, tn), col),
                  pl.BlockSpec((Dp, tn), col),
                  pl.BlockSpec((C, tn), col),
                  pl.BlockSpec((tn, D), lambda q: (pend(q) % nj, 0))],
        out_specs=pl.BlockSpec((tm, D), row(pend)),
        out_shape=jax.ShapeDtypeStruct((T, D), F32),
        scratch_shapes=[pltpu.VMEM((tm, tn), MXU_DTYPE), pltpu.VMEM((tm, tn), MXU_DTYPE),
                        pltpu.VMEM((tm, D), F32)],
        compiler_params=_params("arbitrary"),
        name="merge",
    )(h, a, b, x2, w_ga, w_gb, wa, wb, wo)


def _ffn_kernel(x_ref, g_ref, wg_ref, wu_ref, wd_ref, gf_ref, o_ref, h_ref):
    j = pl.program_id(1)

    @pl.when(j == 0)
    def _():
        x = x_ref[...]
        h_ref[...] = (_rms(x) * g_ref[...]).astype(h_ref.dtype)
        o_ref[...] = x

    h = h_ref[...]
    act = jax.nn.silu(_dot(h, wg_ref[...])) * _dot(h, wu_ref[...])
    o_ref[...] += _dot(act.astype(MXU_DTYPE), wd_ref[...])

    @pl.when(j == pl.num_programs(1) - 1)
    def _():
        o_ref[...] = _rms(o_ref[...]) * gf_ref[...]


def _ffn_call(x2, g, wg, wu, wd, gf):
    T, D = x2.shape
    Dff = wg.shape[1]
    tm = _pick(T, 1024)
    tf = _pick(Dff, 512, LANES)
    return pl.pallas_call(
        _ffn_kernel,
        grid=(T // tm, Dff // tf),
        in_specs=[pl.BlockSpec((tm, D), lambda i, j: (i, 0)),
                  pl.BlockSpec((1, D), lambda i, j: (0, 0)),
                  pl.BlockSpec((D, tf), lambda i, j: (0, j)),
                  pl.BlockSpec((D, tf), lambda i, j: (0, j)),
                  pl.BlockSpec((tf, D), lambda i, j: (j, 0)),
                  pl.BlockSpec((1, D), lambda i, j: (0, 0))],
        out_specs=pl.BlockSpec((tm, D), lambda i, j: (i, 0)),
        out_shape=jax.ShapeDtypeStruct((T, D), F32),
        scratch_shapes=[pltpu.VMEM((tm, D), MXU_DTYPE)],
        compiler_params=_params("parallel", "arbitrary"),
        name="ffn",
    )(x2, g.reshape(1, D), wg, wu, wd, gf.reshape(1, D))


def _trunk(x, w):
    B, L, D = x.shape
    T = B * L
    C = w["hyena_bias"].shape[1]
    assert B % 2 == 0 and L % NF == 0 and C % (2 * LANES) == 0 and w["hyena_bias"].shape[0] == 2
    na = L // NF
    nc = 2 * na
    P = B // 2
    x2 = x.reshape(T, D)

    a, h = _pool_call(x2, w["g_mix"], w["w_pool"], w["pool_w"], w["pool_scale"], L)
    uc = _hyproj_call(h, w["w_hy"], w["conv_w"], w["conv_b"], L, C)
    uc5 = uc.reshape(P, 2, na, NF, 3 * C)

    tabs = _fft_tables(nc)
    two, ss = _filter_call(L, C, *w["filt"])
    spec = _fspec_call(two, ss, w["hyena_bias"].reshape(2, 1, C), tabs, nc)

    z1 = _lconv_call(uc5, 0, uc5, C, spec, 0, tabs, na, F32)
    b = _lconv_call(z1, 0, uc5, 2 * C, spec, 1, tabs, na, MXU_DTYPE).reshape(T, C)

    xn = _merge_call(h, a, b, x2, w["w_ga"], w["w_gb"], w["wa"], w["wb"], w["wo"])
    y = _ffn_call(xn, w["g_ffn"], w["wg"], w["wu"], w["wd"], w["g_final"])
    return y.reshape(B, L, D)


def kernel(x_prompt, x_sample, g_mix, w_in, pool_w, pool_scale, conv_w, conv_b, filt_w1, filt_b1, filt_freq1, filt_w2, filt_b2, filt_freq2, filt_w3, hyena_bias, w_branch_a, w_branch_b, w_out, g_ffn, w_gate, w_up, w_down, g_final):
    assert g_mix.shape[0] == 1, "depth-1 block"
    D = x_prompt.shape[-1]
    Dp = pool_scale.shape[1]
    C = hyena_bias.shape[2]
    s1, s2, s3 = Dp, Dp + 3 * C, Dp + 3 * C + D
    cast = lambda t: t.astype(MXU_DTYPE)
    w = {
        "g_mix": g_mix[0], "w_pool": cast(w_in[0][:, :s1]), "w_hy": cast(w_in[0][:, s1:s2]),
        "w_ga": cast(w_in[0][:, s2:s3]), "w_gb": cast(w_in[0][:, s3:]),
        "pool_w": cast(pool_w[0]), "pool_scale": pool_scale[0],
        "conv_w": conv_w[0], "conv_b": conv_b[0],
        "filt": (filt_w1[0], filt_b1[0], filt_freq1[0], filt_w2[0], filt_b2[0], filt_freq2[0], filt_w3[0]),
        "hyena_bias": hyena_bias[0],
        "wa": cast(w_branch_a[0]), "wb": cast(w_branch_b[0]), "wo": cast(w_out[0]),
        "g_ffn": g_ffn[0], "wg": cast(w_gate[0]), "wu": cast(w_up[0]), "wd": cast(w_down[0]),
        "g_final": g_final,
    }
    return (_trunk(x_prompt, w), _trunk(x_sample, w))
```
